```python
import jax, jax.numpy as jnp
from jax import lax
import numpy as np

D_MODEL = 4096
BATCH = 1
SEQ = 16384
DEPTH = 4

CHUNK = 64
MIX_WIDTH = D_MODEL
CONV_WIDTH = MIX_WIDTH // 2
CONV_GROUPS = 16
CONV_K = 3
GLA_WIDTH = MIX_WIDTH - CONV_WIDTH
GLA_HEADS = 4
GLA_KEY = GLA_WIDTH // 2
GLA_DK = GLA_KEY // GLA_HEADS
GLA_DV = GLA_WIDTH // GLA_HEADS
GATE_RANK = 16
GATE_TAU = 16.0
MEM_LEN = 256
XATTN_HEADS = 4
XATTN_DH = D_MODEL // 16
XATTN_WIDTH = XATTN_HEADS * XATTN_DH
D_FF = 4 * D_MODEL
EPS = 1e-6
SPLITS = (CONV_WIDTH,
          2 * CONV_WIDTH,
          3 * CONV_WIDTH,
          3 * CONV_WIDTH + GLA_KEY,
          3 * CONV_WIDTH + 2 * GLA_KEY,
          3 * CONV_WIDTH + 2 * GLA_KEY + GLA_WIDTH,
          3 * CONV_WIDTH + 2 * GLA_KEY + 2 * GLA_WIDTH)
IN_COLS = 3 * CONV_WIDTH + 2 * GLA_KEY + 2 * GLA_WIDTH + GATE_RANK

kernel_name = 'hybrid_conv_gla_stream_encoder'


def rmsnorm(x, g):
    xf = x.astype(jnp.float32)
    y = xf * lax.rsqrt(jnp.mean(xf * xf, axis=-1, keepdims=True) + EPS)
    return y.astype(x.dtype) * g


def short_conv(b_gate, c_gate, h, w_conv):
    u = c_gate * h
    s = u.shape[1]
    up = jnp.pad(u, ((0, 0), (CONV_K - 1, 0), (0, 0)))
    conv = w_conv[0] * up[:, 0:s]
    for i in range(1, CONV_K):
        conv = conv + w_conv[i] * up[:, i:i + s]
    return b_gate * conv


def gla_chunked(q, k, v, log_a):
    bsz, s = q.shape[:2]
    nc = s // CHUNK

    def to_chunks(t):
        return t.reshape(bsz, nc, CHUNK, GLA_HEADS, t.shape[-1]).transpose(1, 0, 3, 2, 4)

    qc, kc, vc = to_chunks(q), to_chunks(k), to_chunks(v)
    cum = jnp.cumsum(to_chunks(log_a), axis=-2)
    causal = jnp.tril(jnp.ones((CHUNK, CHUNK), dtype=bool))[:, :, None]

    def step(state, inp):
        qi, ki, vi, bi = inp
        o_inter = jnp.einsum('bhld,bhde->bhle', qi * jnp.exp(bi), state)
        diff = bi[:, :, :, None, :] - bi[:, :, None, :, :]
        decay = jnp.exp(jnp.where(causal, diff, -jnp.inf))
        scores = jnp.einsum('bhid,bhjd,bhijd->bhij', qi, ki, decay)
        o_intra = jnp.einsum('bhij,bhje->bhie', scores, vi)
        b_last = bi[:, :, -1:, :]
        k_dec = ki * jnp.exp(b_last - bi)
        new_state = jnp.exp(b_last)[:, :, 0, :, None] * state + jnp.einsum('bhld,bhle->bhde', k_dec, vi)
        return new_state, o_inter + o_intra

    s0 = jnp.zeros((bsz, GLA_HEADS, GLA_DK, GLA_DV), jnp.float32)
    _, out = lax.scan(step, s0, (qc, kc, vc, cum))
    return out.transpose(1, 0, 3, 2, 4).reshape(bsz, s, GLA_HEADS, GLA_DV)


def hybrid_mixer(xn, w_in, w_conv, w_gate_up, b_gate, gla_norm, w_out):
    bsz, s, _ = xn.shape
    f32 = jnp.float32
    proj = xn @ w_in
    b_c, c_c, h_c, q, k, v, g, a_lr = jnp.split(proj, SPLITS, axis=-1)
    y_conv = short_conv(b_c, c_c, h_c, w_conv)
    log_a = jax.nn.log_sigmoid(a_lr.astype(f32) @ w_gate_up.astype(f32) + b_gate.astype(f32)) / GATE_TAU
    qh = q.astype(f32).reshape(bsz, s, GLA_HEADS, GLA_DK) * (GLA_DK ** -0.5)
    kh = k.astype(f32).reshape(bsz, s, GLA_HEADS, GLA_DK)
    vh = v.astype(f32).reshape(bsz, s, GLA_HEADS, GLA_DV)
    o = gla_chunked(qh, kh, vh, log_a.reshape(bsz, s, GLA_HEADS, GLA_DK))
    o = o * lax.rsqrt(jnp.mean(o * o, axis=-1, keepdims=True) + EPS)
    o = o.reshape(bsz, s, GLA_WIDTH).astype(xn.dtype) * gla_norm * jax.nn.silu(g)
    return jnp.concatenate([y_conv, o], axis=-1) @ w_out


def mem_cross_attention(xn, memn, wq, wk, wv, wo):
    bsz, s, _ = xn.shape
    q = (xn @ wq).reshape(bsz, s, XATTN_HEADS, XATTN_DH)
    k = (memn @ wk).reshape(bsz, MEM_LEN, XATTN_HEADS, XATTN_DH)
    v = (memn @ wv).reshape(bsz, MEM_LEN, XATTN_HEADS, XATTN_DH)
    sc = jnp.einsum('bshd,bmhd->bhsm', q, k).astype(jnp.float32) * (XATTN_DH ** -0.5)
    p = jax.nn.softmax(sc, axis=-1).astype(v.dtype)
    o = jnp.einsum('bhsm,bmhd->bshd', p, v).reshape(bsz, s, XATTN_WIDTH)
    return o @ wo


def sq_relu_mlp(xn, w_up, w_down):
    h = jax.nn.relu(xn @ w_up)
    return (h * h) @ w_down


def _normal(key, shape, scale):
    return jax.random.normal(key, shape, jnp.float32) * scale


def setup_inputs(seed: int = 0) -> dict:
    key = jax.random.key(seed)
    ks = jax.random.split(key, 20)
    return {
        'x': _normal(ks[0], (BATCH, SEQ, D_MODEL), 1.0),
        'mem': _normal(ks[1], (BATCH, MEM_LEN, D_MODEL), 1.0),
        'norm_mix': 1.0 + _normal(ks[2], (DEPTH, D_MODEL), 0.02),
        'w_in': _normal(ks[3], (DEPTH, D_MODEL, IN_COLS), D_MODEL ** -0.5),
        'w_conv': _normal(ks[4], (DEPTH, CONV_K, CONV_WIDTH), CONV_K ** -0.5),
        'w_gate_up': _normal(ks[5], (DEPTH, GATE_RANK, GLA_KEY), GATE_RANK ** -0.5),
        'b_gate': _normal(ks[6], (DEPTH, GLA_KEY), 0.1),
        'gla_norm': 1.0 + _normal(ks[7], (DEPTH, GLA_WIDTH), 0.02),
        'w_out': _normal(ks[8], (DEPTH, MIX_WIDTH, D_MODEL), MIX_WIDTH ** -0.5),
        'norm_xattn': 1.0 + _normal(ks[9], (DEPTH, D_MODEL), 0.02),
        'norm_mem': 1.0 + _normal(ks[10], (DEPTH, D_MODEL), 0.02),
        'wq_x': _normal(ks[11], (DEPTH, D_MODEL, XATTN_WIDTH), D_MODEL ** -0.5),
        'wk_x': _normal(ks[12], (DEPTH, D_MODEL, XATTN_WIDTH), D_MODEL ** -0.5),
        'wv_x': _normal(ks[13], (DEPTH, D_MODEL, XATTN_WIDTH), D_MODEL ** -0.5),
        'wo_x': _normal(ks[14], (DEPTH, XATTN_WIDTH, D_MODEL), XATTN_WIDTH ** -0.5),
        'norm_mlp': 1.0 + _normal(ks[15], (DEPTH, D_MODEL), 0.02),
        'w_up': _normal(ks[16], (DEPTH, D_MODEL, D_FF), D_MODEL ** -0.5),
        'w_down': _normal(ks[17], (DEPTH, D_FF, D_MODEL), D_FF ** -0.5),
        'norm_final': 1.0 + _normal(ks[18], (D_MODEL,), 0.02),
    }


def reference(x, mem, norm_mix, w_in, w_conv, w_gate_up, b_gate, gla_norm, w_out,
              norm_xattn, norm_mem, wq_x, wk_x, wv_x, wo_x, norm_mlp, w_up, w_down, norm_final):
    h = x
    for l in range(DEPTH):
        h = h + hybrid_mixer(rmsnorm(h, norm_mix[l]), w_in[l], w_conv[l], w_gate_up[l],
                             b_gate[l], gla_norm[l], w_out[l])
        h = h + mem_cross_attention(rmsnorm(h, norm_xattn[l]), rmsnorm(mem, norm_mem[l]),
                                    wq_x[l], wk_x[l], wv_x[l], wo_x[l])
        h = h + sq_relu_mlp(rmsnorm(h, norm_mlp[l]), w_up[l], w_down[l])
    return rmsnorm(h, norm_final)
```

```python
import functools

import jax
import jax.numpy as jnp
from jax import lax
from jax.experimental import pallas as pl
from jax.experimental.pallas import tpu as pltpu

EPS = 1e-6
GLA_CHUNK = 64
GLA_SUB = 16
GLA_HEADS = 4
XATTN_HEADS = 4
GATE_TAU = 16.0
CONV_K = 3
LANES = 128
SUBLANES = 8
VMEM_LIMIT = 56 * 1024 * 1024

_BF16 = jnp.bfloat16
_F32 = jnp.float32


def _params(sem):
    return pltpu.CompilerParams(dimension_semantics=sem, vmem_limit_bytes=VMEM_LIMIT)


def _rmsnorm_body(x_ref, g_ref, o_ref):
    x = x_ref[...]
    y = x * lax.rsqrt(jnp.mean(x * x, axis=-1, keepdims=True) + EPS)
    o_ref[...] = (y * g_ref[...]).astype(o_ref.dtype)


def _rmsnorm_gate_body(x_ref, g_ref, wa_ref, o_ref, a_ref):
    x = x_ref[...]
    y = x * lax.rsqrt(jnp.mean(x * x, axis=-1, keepdims=True) + EPS)
    xn = (y * g_ref[...]).astype(o_ref.dtype)
    o_ref[...] = xn
    a_ref[...] = jnp.dot(xn, wa_ref[...], preferred_element_type=_F32)


def _rmsnorm(x, gain, out_dtype, bm, w_a=None):
    m, d = x.shape
    bm = min(bm, m)
    grid = (m // bm,)
    x_spec = pl.BlockSpec((bm, d), lambda i: (i, 0))
    g_spec = pl.BlockSpec((1, d), lambda i: (0, 0))
    gain2 = gain.reshape(1, d).astype(_F32)
    if w_a is None:
        return pl.pallas_call(
            _rmsnorm_body,
            out_shape=jax.ShapeDtypeStruct((m, d), out_dtype),
            grid=grid,
            in_specs=[x_spec, g_spec],
            out_specs=pl.BlockSpec((bm, d), lambda i: (i, 0)),
            compiler_params=_params(("parallel",)),
            name="rmsnorm",
        )(x, gain2)
    na = w_a.shape[1]
    return pl.pallas_call(
        _rmsnorm_gate_body,
        out_shape=(jax.ShapeDtypeStruct((m, d), out_dtype),
                   jax.ShapeDtypeStruct((m, na), _F32)),
        grid=grid,
        in_specs=[x_spec, g_spec, pl.BlockSpec((d, na), lambda i: (0, 0))],
        out_specs=(pl.BlockSpec((bm, d), lambda i: (i, 0)),
                   pl.BlockSpec((bm, na), lambda i: (i, 0))),
        compiler_params=_params(("parallel",)),
        name="rmsnorm_gate",
    )(x, gain2, w_a)


def _mm_body(*refs, npairs, nk, act, has_res):
    xs = refs[:npairs]
    ws = refs[npairs:2 * npairs]
    pos = 2 * npairs
    res_ref = refs[pos] if has_res else None
    pos += int(has_res)
    o_ref = refs[pos]
    acc_ref = refs[pos + 1] if nk > 1 else None

    part = jnp.dot(xs[0][...], ws[0][...], preferred_element_type=_F32)
    for p in range(1, npairs):
        part = part + jnp.dot(xs[p][...], ws[p][...], preferred_element_type=_F32)

    def finish(acc):
        if act == "relu2":
            r = jnp.maximum(acc, 0.0)
            acc = r * r
        if has_res:
            acc = acc + res_ref[...]
        o_ref[...] = acc.astype(o_ref.dtype)

    if nk == 1:
        finish(part)
        return

    k = pl.program_id(2)

    @pl.when(k == 0)
    def _():
        acc_ref[...] = part

    @pl.when(jnp.logical_and(k > 0, k < nk - 1))
    def _():
        acc_ref[...] += part

    @pl.when(k == nk - 1)
    def _():
        finish(acc_ref[...] + part)


def _matmul(xs, ws, *, out_dtype, res=None, act=None, bm=1024, bn=1024, bk=2048, name="matmul"):
    npairs = len(xs)
    m, kdim = xs[0].shape
    n = ws[0].shape[1]
    bm, bn, bk = min(bm, m), min(bn, n), min(bk, kdim)
    nk = kdim // bk
    assert m % bm == 0 and n % bn == 0 and kdim % bk == 0
    grid = (m // bm, n // bn, nk)
    in_specs = ([pl.BlockSpec((bm, bk), lambda i, j, k: (i, k))] * npairs
                + [pl.BlockSpec((bk, bn), lambda i, j, k: (k, j))] * npairs)
    args = list(xs) + list(ws)
    if res is not None:
        in_specs.append(pl.BlockSpec((bm, bn), lambda i, j, k: (i, j)))
        args.append(res)
    scratch = [pltpu.VMEM((bm, bn), _F32)] if nk > 1 else []
    body = functools.partial(_mm_body, npairs=npairs, nk=nk, act=act, has_res=res is not None)
    return pl.pallas_call(
        body,
        out_shape=jax.ShapeDtypeStruct((m, n), out_dtype),
        grid=grid,
        in_specs=in_specs,
        out_specs=pl.BlockSpec((bm, bn), lambda i, j, k: (i, j)),
        scratch_shapes=scratch,
        compiler_params=_params(("parallel", "parallel", "arbitrary")),
        name=name,
    )(*args)


def _conv_body(b_ref, c_ref, h_ref, w_ref, o_ref, tail_ref):
    t = pl.program_id(1)

    @pl.when(t == 0)
    def _():
        tail_ref[...] = jnp.zeros_like(tail_ref)

    u = c_ref[...].astype(_F32) * h_ref[...].astype(_F32)
    rows = u.shape[0]
    row = lax.broadcasted_iota(jnp.int32, u.shape, 0)
    prev1 = tail_ref[SUBLANES - 1:SUBLANES, :]
    prev2 = tail_ref[SUBLANES - 2:SUBLANES - 1, :]
    u1 = jnp.where(row == 0, prev1, pltpu.roll(u, 1, axis=0))
    u2 = jnp.where(row == 0, prev2, jnp.where(row == 1, prev1, pltpu.roll(u, 2, axis=0)))
    w = w_ref[...]
    conv = w[0:1, :] * u2 + w[1:2, :] * u1 + w[2:3, :] * u
    o_ref[...] = (b_ref[...].astype(_F32) * conv).astype(o_ref.dtype)
    tail_ref[...] = u[rows - SUBLANES:, :]


def _short_conv(proj, w_conv, *, width, tb=1024, cb=512):
    s = proj.shape[0]
    ncb = width // cb
    w_pad = jnp.zeros((SUBLANES, width), _F32).at[:CONV_K].set(w_conv.astype(_F32))
    return pl.pallas_call(
        _conv_body,
        out_shape=jax.ShapeDtypeStruct((s, width), _BF16),
        grid=(ncb, s // tb),
        in_specs=[pl.BlockSpec((tb, cb), lambda c, t: (t, c)),
                  pl.BlockSpec((tb, cb), lambda c, t: (t, ncb + c)),
                  pl.BlockSpec((tb, cb), lambda c, t: (t, 2 * ncb + c)),
                  pl.BlockSpec((SUBLANES, cb), lambda c, t: (0, c))],
        out_specs=pl.BlockSpec((tb, cb), lambda c, t: (t, c)),
        scratch_shapes=[pltpu.VMEM((SUBLANES, cb), _F32)],
        compiler_params=_params(("parallel", "arbitrary")),
        name="short_conv",
    )(proj, proj, proj, w_pad)


def _bf16_split_dot(a_exact_bf16, x):
    hi = x.astype(_BF16)
    lo = (x - hi.astype(_F32)).astype(_BF16)
    return (jnp.dot(a_exact_bf16, hi, preferred_element_type=_F32)
            + jnp.dot(a_exact_bf16, lo, preferred_element_type=_F32))


def _gla_body(q_ref, k_ref, v_ref, g_ref, a_ref, wg_ref, bg_ref, gn_ref, o_ref,
              state_ref, la_ref, *, dk, dv, tb):
    L, C = GLA_CHUNK, GLA_SUB
    nsub = L // C

    @pl.when(pl.program_id(1) == 0)
    def _():
        state_ref[...] = jnp.zeros_like(state_ref)

    z = jnp.dot(a_ref[...].astype(_BF16), wg_ref[...], preferred_element_type=_F32) + bg_ref[...]
    la_ref[...] = (jnp.minimum(z, 0.0) - jnp.log1p(jnp.exp(-jnp.abs(z)))) * (1.0 / GATE_TAU)

    ri = lax.broadcasted_iota(jnp.int32, (L, L), 0)
    ci = lax.broadcasted_iota(jnp.int32, (L, L), 1)
    tri_incl = (ci <= ri).astype(_BF16)
    blk_r = ri // C
    blk_c = ci // C
    lane_l = lax.broadcasted_iota(jnp.int32, (C, L), 1)
    row_l = lax.broadcasted_iota(jnp.int32, (C, L), 0)
    scale = dk ** -0.5
    gn = gn_ref[...]

    def chunk(c, carry):
        r0 = pl.multiple_of(c * L, L)
        q = q_ref[pl.ds(r0, L), :].astype(_F32) * scale
        k = k_ref[pl.ds(r0, L), :].astype(_F32)
        v = v_ref[pl.ds(r0, L), :]
        la = la_ref[pl.ds(r0, L), :]
        b = _bf16_split_dot(tri_incl, la)
        b_last = b[L - 1:L, :]

        st = state_ref[...]
        q_in = (q * jnp.exp(b)).astype(_BF16)
        o = lax.dot_general(q_in, st.astype(_BF16), (((1,), (1,)), ((), ())),
                            preferred_element_type=_F32)

        b_end = jnp.concatenate(
            [jnp.broadcast_to(b[(j + 1) * C - 1:(j + 1) * C, :], (C, dk)) for j in range(nsub)], axis=0)
        k_hat = (k * jnp.exp(b_end - b)).astype(_BF16)
        q_hat = jnp.concatenate(
            [(q * jnp.exp(jnp.minimum(b - b[(j + 1) * C - 1:(j + 1) * C, :], 0.0))).astype(_BF16)
             for j in range(nsub - 1)], axis=0)
        p_off = lax.dot_general(q_hat, k_hat, (((1,), (1,)), ((), ())),
                                preferred_element_type=_F32)
        a_mat = jnp.zeros((L, L), _F32)
        for j in range(nsub - 1):
            sel = jnp.logical_and(blk_c == j, blk_r > j)
            a_mat = jnp.where(sel, p_off[j * L:(j + 1) * L, :], a_mat)

        diag_rows = []
        for i_blk in range(nsub):
            lo = i_blk * C
            q_i = q[lo:lo + C, :]
            k_i = k[lo:lo + C, :]
            b_i = b[lo:lo + C, :]
            acc = jnp.zeros((C, L), _F32)
            for j in range(C):
                w = jnp.exp(jnp.minimum(b_i - b_i[j:j + 1, :], 0.0))
                s_col = jnp.sum(q_i * k_i[j:j + 1, :] * w, axis=-1, keepdims=True)
                acc = jnp.where(lane_l == lo + j, s_col, acc)
            keep = jnp.logical_and(lane_l - lo <= row_l, lane_l >= lo)
            diag_rows.append(jnp.where(keep, acc, 0.0))
        a_mat = a_mat + jnp.concatenate(diag_rows, axis=0)
        o = o + jnp.dot(a_mat.astype(_BF16), v, preferred_element_type=_F32)

        k_dec = (k * jnp.exp(b_last - b)).astype(_BF16)
        upd = lax.dot_general(v, k_dec, (((0,), (0,)), ((), ())), preferred_element_type=_F32)
        state_ref[...] = st * jnp.exp(b_last) + upd

        o = o * lax.rsqrt(jnp.mean(o * o, axis=-1, keepdims=True) + EPS)
        g = g_ref[pl.ds(r0, L), :].astype(_F32)
        o_ref[pl.ds(r0, L), :] = (o * gn * (g * jax.nn.sigmoid(g))).astype(o_ref.dtype)
        return carry

    lax.fori_loop(0, tb // L, chunk, 0)


def _gla(proj, a_lr, w_gate_pad, b_gate, gla_norm, *, conv_width, dk, dv, tb=512):
    s = proj.shape[0]
    h = GLA_HEADS
    q_blk0 = (3 * conv_width) // dk
    k_blk0 = q_blk0 + h
    v_blk0 = (3 * conv_width + 2 * h * dk) // dv
    g_blk0 = v_blk0 + h
    na = a_lr.shape[1]
    body = functools.partial(_gla_body, dk=dk, dv=dv, tb=tb)
    return pl.pallas_call(
        body,
        out_shape=jax.ShapeDtypeStruct((s, h * dv), _BF16),
        grid=(h, s // tb),
        in_specs=[pl.BlockSpec((tb, dk), lambda hd, t: (t, q_blk0 + hd)),
                  pl.BlockSpec((tb, dk), lambda hd, t: (t, k_blk0 + hd)),
                  pl.BlockSpec((tb, dv), lambda hd, t: (t, v_blk0 + hd)),
                  pl.BlockSpec((tb, dv), lambda hd, t: (t, g_blk0 + hd)),
                  pl.BlockSpec((tb, na), lambda hd, t: (t, 0)),
                  pl.BlockSpec((na, dk), lambda hd, t: (0, hd)),
                  pl.BlockSpec((1, dk), lambda hd, t: (0, hd)),
                  pl.BlockSpec((1, dv), lambda hd, t: (0, hd))],
        out_specs=pl.BlockSpec((tb, dv), lambda hd, t: (t, hd)),
        scratch_shapes=[pltpu.VMEM((dv, dk), _F32), pltpu.VMEM((tb, dk), _F32)],
        compiler_params=_params(("parallel", "arbitrary")),
        name="gla",
    )(proj, proj, proj, proj, a_lr, w_gate_pad, b_gate.reshape(1, -1).astype(_F32),
      gla_norm.reshape(1, -1).astype(_F32))


def _xattn_body(q_ref, k_ref, v_ref, o_ref, *, dh):
    scale = dh ** -0.5
    for hd in range(XATTN_HEADS):
        sl = slice(hd * dh, (hd + 1) * dh)
        sc = lax.dot_general(q_ref[:, sl], k_ref[:, sl], (((1,), (1,)), ((), ())),
                             preferred_element_type=_F32) * scale
        p = jnp.exp(sc - jnp.max(sc, axis=-1, keepdims=True))
        denom = jnp.sum(p, axis=-1, keepdims=True)
        pv = jnp.dot(p.astype(_BF16), v_ref[:, sl], preferred_element_type=_F32)
        o_ref[:, sl] = (pv / denom).astype(o_ref.dtype)


def _xattn(q, k, v, *, bm=1024):
    s, width = q.shape
    mlen = k.shape[0]
    body = functools.partial(_xattn_body, dh=width // XATTN_HEADS)
    return pl.pallas_call(
        body,
        out_shape=jax.ShapeDtypeStruct((s, width), _BF16),
        grid=(s // bm,),
        in_specs=[pl.BlockSpec((bm, width), lambda i: (i, 0)),
                  pl.BlockSpec((mlen, width), lambda i: (0, 0)),
                  pl.BlockSpec((mlen, width), lambda i: (0, 0))],
        out_specs=pl.BlockSpec((bm, width), lambda i: (i, 0)),
        compiler_params=_params(("parallel",)),
        name="xattn",
    )(q, k, v)


def kernel(x, mem, norm_mix, w_in, w_conv, w_gate_up, b_gate, gla_norm, w_out, norm_xattn, norm_mem,
           wq_x, wk_x, wv_x, wo_x, norm_mlp, w_up, w_down, norm_final):
    bsz, seq, d = x.shape
    depth = w_in.shape[0]
    conv_width = w_conv.shape[2]
    gla_key = w_gate_up.shape[2]
    gla_width = gla_norm.shape[1]
    gate_rank = w_gate_up.shape[1]
    dk = gla_key // GLA_HEADS
    dv = gla_width // GLA_HEADS
    main_cols = 3 * conv_width + 2 * gla_key + 2 * gla_width
    assert bsz == 1 and w_in.shape[2] == main_cols + gate_rank

    h = x.reshape(seq, d)
    mem2 = mem.reshape(mem.shape[1], d)
    for l in range(depth):
        w_main = w_in[l, :, :main_cols].astype(_BF16)
        w_a = jnp.zeros((d, LANES), _BF16).at[:, :gate_rank].set(w_in[l, :, main_cols:].astype(_BF16))
        w_gate_pad = jnp.zeros((LANES, gla_key), _BF16).at[:gate_rank].set(w_gate_up[l].astype(_BF16))
        xn, a_lr = _rmsnorm(h, norm_mix[l], _BF16, 512, w_a=w_a)
        proj = _matmul([xn], [w_main], out_dtype=_BF16, name="in_proj")
        y_conv = _short_conv(proj, w_conv[l], width=conv_width)
        o_gla = _gla(proj, a_lr, w_gate_pad, b_gate[l], gla_norm[l], conv_width=conv_width, dk=dk, dv=dv)
        w_o = w_out[l].astype(_BF16)
        h = _matmul([y_conv, o_gla], [w_o[:conv_width], w_o[conv_width:]], out_dtype=_F32, res=h,
                    name="out_proj")
        xn = _rmsnorm(h, norm_xattn[l], _BF16, 512)
        memn = _rmsnorm(mem2, norm_mem[l], _BF16, 256)
        q = _matmul([xn], [wq_x[l].astype(_BF16)], out_dtype=_BF16, name="xattn_q")
        k = _matmul([memn], [wk_x[l].astype(_BF16)], out_dtype=_BF16, name="xattn_k")
        v = _matmul([memn], [wv_x[l].astype(_BF16)], out_dtype=_BF16, name="xattn_v")
        att = _xattn(q, k, v)
        h = _matmul([att], [wo_x[l].astype(_BF16)], out_dtype=_F32, res=h, name="xattn_o")
        xn = _rmsnorm(h, norm_mlp[l], _BF16, 512)
        hid = _matmul([xn], [w_up[l].astype(_BF16)], out_dtype=_BF16, act="relu2", name="mlp_up")
        h = _matmul([hid], [w_down[l].astype(_BF16)], out_dtype=_F32, res=h, name="mlp_down")
    out = _rmsnorm(h, norm_final, _F32, 512)
    return out.reshape(bsz, seq, d)
```

```python
import functools
import math

import jax
import jax.numpy as jnp
import numpy as np
from jax import lax
from jax.experimental import pallas as pl
from jax.experimental.pallas import tpu as pltpu

EPS = 1e-6
GLA_CHUNK = 64
GLA_GROUP = 4
GLA_HEADS = 4
XATTN_HEADS = 4
GATE_TAU = 16.0
CONV_K = 3
LANES = 128
SUBLANES = 8
MIB = 1024 * 1024
VMEM_CAP = 60 * MIB
VMEM_TEMPS = 12 * MIB
VMEM_DEFAULT = 40 * MIB
LOG2E = math.log2(math.e)

_BF16 = jnp.bfloat16
_F32 = jnp.float32
_NT = (((1,), (1,)), ((), ()))
_TN = (((0,), (0,)), ((), ()))


def _params(sem, block_bytes=None):
    limit = VMEM_DEFAULT if block_bytes is None else min(VMEM_CAP, block_bytes + VMEM_TEMPS)
    return pltpu.CompilerParams(dimension_semantics=sem, vmem_limit_bytes=limit)


def _rmsnorm_body(x_ref, g_ref, o_ref):
    x = x_ref[...]
    y = x * lax.rsqrt(jnp.mean(x * x, axis=-1, keepdims=True) + EPS)
    o_ref[...] = (y * g_ref[...]).astype(o_ref.dtype)


def _rmsnorm_gate_body(x_ref, g_ref, wa_ref, o_ref, a_ref):
    x = x_ref[...]
    y = x * lax.rsqrt(jnp.mean(x * x, axis=-1, keepdims=True) + EPS)
    xn = (y * g_ref[...]).astype(o_ref.dtype)
    o_ref[...] = xn
    a_ref[...] = jnp.dot(xn, wa_ref[...], preferred_element_type=_F32)


def _rmsnorm(x, gain, out_dtype, bm, w_a=None):
    m, d = x.shape
    bm = min(bm, m)
    grid = (m // bm,)
    x_spec = pl.BlockSpec((bm, d), lambda i: (i, 0))
    g_spec = pl.BlockSpec((1, d), lambda i: (0, 0))
    gain2 = gain.reshape(1, d).astype(_F32)
    if w_a is None:
        return pl.pallas_call(
            _rmsnorm_body,
            out_shape=jax.ShapeDtypeStruct((m, d), out_dtype),
            grid=grid,
            in_specs=[x_spec, g_spec],
            out_specs=pl.BlockSpec((bm, d), lambda i: (i, 0)),
            compiler_params=_params(("parallel",)),
            name="rmsnorm",
        )(x, gain2)
    na = w_a.shape[1]
    return pl.pallas_call(
        _rmsnorm_gate_body,
        out_shape=(jax.ShapeDtypeStruct((m, d), out_dtype),
                   jax.ShapeDtypeStruct((m, na), _F32)),
        grid=grid,
        in_specs=[x_spec, g_spec, pl.BlockSpec((d, na), lambda i: (0, 0))],
        out_specs=(pl.BlockSpec((bm, d), lambda i: (i, 0)),
                   pl.BlockSpec((bm, na), lambda i: (i, 0))),
        compiler_params=_params(("parallel",)),
        name="rmsnorm_gate",
    )(x, gain2, w_a)


def _mm_body(*refs, npairs, nk, act, has_res):
    xs = refs[:npairs]
    ws = refs[npairs:2 * npairs]
    res_ref = refs[2 * npairs] if has_res else None
    o_ref = refs[2 * npairs + int(has_res)]

    part = jnp.dot(xs[0][...], ws[0][...], preferred_element_type=_F32)
    for p in range(1, npairs):
        part = part + jnp.dot(xs[p][...], ws[p][...], preferred_element_type=_F32)

    if nk == 1:
        if act == "relu2":
            r = jnp.maximum(part, 0.0)
            part = r * r
        if has_res:
            part = part + res_ref[...]
        o_ref[...] = part.astype(o_ref.dtype)
        return

    @pl.when(pl.program_id(2) == 0)
    def _():
        o_ref[...] = res_ref[...] if has_res else jnp.zeros_like(o_ref)

    o_ref[...] += part


def _matmul(xs, ws, layer, *, out_dtype, res=None, act=None, bm=1024, bn=1024, bk=4096, name="matmul"):
    npairs = len(xs)
    m, kdim = xs[0].shape
    n = ws[0][0].shape[2]
    bm, bn, bk = min(bm, m), min(bn, n), min(bk, kdim)
    nk = kdim // bk
    assert m % bm == 0 and kdim % bk == 0
    assert nk == 1 or (act is None and out_dtype == _F32)
    grid = (m // bm, n // bn, nk)
    in_specs = [pl.BlockSpec((bm, bk), lambda i, j, k: (i, k))] * npairs
    for _, row0 in ws:
        assert row0 % bk == 0
        in_specs.append(pl.BlockSpec((None, bk, bn), lambda i, j, k, kb=row0 // bk: (layer, kb + k, j)))
    args = list(xs) + [w for w, _ in ws]
    if res is not None:
        in_specs.append(pl.BlockSpec((bm, bn), lambda i, j, k: (i, j)))
        args.append(res)
    body = functools.partial(_mm_body, npairs=npairs, nk=nk, act=act, has_res=res is not None)
    block_bytes = 2 * (npairs * 2 * (bm * bk + bk * bn) + bm * bn * jnp.dtype(out_dtype).itemsize
                       + (4 * bm * bn if res is not None else 0))
    return pl.pallas_call(
        body,
        out_shape=jax.ShapeDtypeStruct((m, grid[1] * bn), out_dtype),
        grid=grid,
        in_specs=in_specs,
        out_specs=pl.BlockSpec((bm, bn), lambda i, j, k: (i, j)),
        compiler_params=_params(("parallel", "parallel", "arbitrary"), block_bytes),
        name=name,
    )(*args)


def _conv_body(b_ref, c_ref, h_ref, w_ref, o_ref, tail_ref):
    t = pl.program_id(1)

    @pl.when(t == 0)
    def _():
        tail_ref[...] = jnp.zeros_like(tail_ref)

    u = c_ref[...].astype(_F32) * h_ref[...].astype(_F32)
    rows = u.shape[0]
    row = lax.broadcasted_iota(jnp.int32, u.shape, 0)
    prev1 = tail_ref[SUBLANES - 1:SUBLANES, :]
    prev2 = tail_ref[SUBLANES - 2:SUBLANES - 1, :]
    u1 = jnp.where(row == 0, prev1, pltpu.roll(u, 1, axis=0))
    u2 = jnp.where(row == 0, prev2, jnp.where(row == 1, prev1, pltpu.roll(u, 2, axis=0)))
    w = w_ref[...]
    conv = w[0:1, :] * u2 + w[1:2, :] * u1 + w[2:3, :] * u
    o_ref[...] = (b_ref[...].astype(_F32) * conv).astype(o_ref.dtype)
    tail_ref[...] = u[rows - SUBLANES:, :]


def _short_conv(proj, w_conv, *, width, tb=1024, cb=512):
    s = proj.shape[0]
    ncb = width // cb
    w_pad = jnp.zeros((SUBLANES, width), _F32).at[:CONV_K].set(w_conv.astype(_F32))
    return pl.pallas_call(
        _conv_body,
        out_shape=jax.ShapeDtypeStruct((s, width), _BF16),
        grid=(ncb, s // tb),
        in_specs=[pl.BlockSpec((tb, cb), lambda c, t: (t, c)),
                  pl.BlockSpec((tb, cb), lambda c, t: (t, ncb + c)),
                  pl.BlockSpec((tb, cb), lambda c, t: (t, 2 * ncb + c)),
                  pl.BlockSpec((SUBLANES, cb), lambda c, t: (0, c))],
        out_specs=pl.BlockSpec((tb, cb), lambda c, t: (t, c)),
        scratch_shapes=[pltpu.VMEM((SUBLANES, cb), _F32)],
        compiler_params=_params(("parallel", "arbitrary")),
        name="short_conv",
    )(proj, proj, proj, w_pad)


def _bf16_split_dot(a_exact_bf16, x):
    hi = x.astype(_BF16)
    lo = (x - hi.astype(_F32)).astype(_BF16)
    return (jnp.dot(a_exact_bf16, hi, preferred_element_type=_F32)
            + jnp.dot(a_exact_bf16, lo, preferred_element_type=_F32))


def _gla_constants(L, G):
    nlev = L.bit_length() - 1
    r = np.arange(L * G)
    cum_op = (r[None, :] <= r[:, None]) & (r[None, :] // L == r[:, None] // L)
    i = np.arange(L)
    base = [np.eye(L, dtype=bool)]
    for lev in range(nlev):
        s = L >> lev
        lower = (i % s) >= s // 2
        base.append((i[:, None] // s == i[None, :] // s) & lower[:, None] & ~lower[None, :])
    base = np.stack(base).astype(np.float32)
    masks = np.zeros((nlev + 1, 2, L, 2 * L), np.float32)
    masks[:, 0, :, :L] = base
    masks[:, 1, :, L:] = base
    return jnp.asarray(cum_op, _BF16), jnp.asarray(masks)


def _gla_level_operand(q, k, b, s, row8):
    half = s // 2
    n = b.shape[0]
    parts = []
    if half >= SUBLANES:
        for r0 in range(0, n, half):
            rows = slice(r0, r0 + half)
            r_bound = (r0 // s) * s + half - 1
            b_r = b[r_bound:r_bound + 1, :]
            if (r0 // half) % 2:
                parts.append(q[rows] * jnp.exp2(b[rows] - b_r))
            else:
                parts.append(k[rows] * jnp.exp2(b_r - b[rows]))
    else:
        low = (row8 & (s - 1)) >= half
        sign = jnp.where(low, -1.0, 1.0)
        for r0 in range(0, n, SUBLANES):
            rows = slice(r0, r0 + SUBLANES)
            b8 = b[rows]
            if half == 1:
                b_r = jnp.where(low, pltpu.roll(b8, 1, axis=0), b8)
            else:
                b_r = b8[half - 1:half, :]
                for blk in range(1, SUBLANES // s):
                    b_r = jnp.where(row8 >= blk * s, b8[blk * s + half - 1:blk * s + half, :], b_r)
            parts.append(jnp.where(low, q[rows], k[rows]) * jnp.exp2((b_r - b8) * sign))
    return jnp.concatenate(parts, axis=0).astype(_BF16)


def _gla_prepare_head(q, k, v, la, cum_op, mask_ref, row8):
    L, G = GLA_CHUNK, GLA_GROUP
    nlev = L.bit_length() - 1
    dk = q.shape[1]
    b = _bf16_split_dot(cum_op, la)
    b_last = [b[(c + 1) * L - 1:(c + 1) * L, :] for c in range(G)]
    q_in = (q * jnp.exp2(b)).astype(_BF16)
    k_dec = jnp.concatenate(
        [k[c * L:(c + 1) * L] * jnp.exp2(b_last[c] - b[c * L:(c + 1) * L]) for c in range(G)], axis=0
    ).astype(_BF16)
    decays = [jnp.exp2(bl) for bl in b_last]

    acc = [None] * G

    def add_level(idx, p):
        for c in range(G):
            t = c // 2
            blk = p[c * L:(c + 1) * L, t * LANES:(t + 1) * LANES] * mask_ref[idx, c % 2]
            acc[c] = blk if acc[c] is None else acc[c] + blk

    add_level(0, lax.dot_general(q.astype(_BF16), k.astype(_BF16), _NT, preferred_element_type=_F32))
    for lev in range(nlev):
        x = _gla_level_operand(q, k, b, L >> lev, row8)
        add_level(lev + 1, lax.dot_general(x, x, _NT, preferred_element_type=_F32))

    zeros = jnp.zeros((L, LANES), _F32)
    a_bd = jnp.concatenate(
        [jnp.concatenate([acc[c] if t == c // 2 else zeros for t in range(G // 2)], axis=1) for c in range(G)],
        axis=0).astype(_BF16)
    o_intra = jnp.dot(a_bd, v, preferred_element_type=_F32)
    return o_intra, q_in, k_dec, decays


def _gla_body(q_ref, k_ref, v_ref, g_ref, a_ref, wg_ref, bg_ref, gn_ref, cum_ref, mask_ref, o_ref,
              state_ref, oin_ref, qin_ref, kdec_ref, dec_ref, *, dk, dv):
    L, G = GLA_CHUNK, GLA_GROUP

    @pl.when(pl.program_id(0) == 0)
    def _():
        state_ref[...] = jnp.zeros_like(state_ref)

    z = jnp.dot(a_ref[...].astype(_BF16), wg_ref[...], preferred_element_type=_F32) + bg_ref[...]
    la_all = (jnp.minimum(z, 0.0) - jnp.log(1.0 + jnp.exp(-jnp.abs(z)))) * (LOG2E / GATE_TAU)

    row8 = lax.broadcasted_iota(jnp.int32, (SUBLANES, dk), 0)
    scale = dk ** -0.5
    cum_op = cum_ref[...]
    for hd in range(GLA_HEADS):
        kc = slice(hd * dk, (hd + 1) * dk)
        vc = slice(hd * dv, (hd + 1) * dv)
        o_intra, q_in, k_dec, decays = _gla_prepare_head(
            q_ref[:, kc].astype(_F32) * scale, k_ref[:, kc].astype(_F32), v_ref[:, vc], la_all[:, kc],
            cum_op, mask_ref, row8)
        oin_ref[:, vc] = o_intra
        qin_ref[:, kc] = q_in
        kdec_ref[:, kc] = k_dec
        for c in range(G):
            dec_ref[c:c + 1, kc] = decays[c]

    for c in range(G):
        rows = slice(c * L, (c + 1) * L)
        for hd in range(GLA_HEADS):
            kc = slice(hd * dk, (hd + 1) * dk)
            vc = slice(hd * dv, (hd + 1) * dv)
            st = state_ref[hd]
            o = oin_ref[rows, vc] + lax.dot_general(qin_ref[rows, kc], st.astype(_BF16), _NT,
                                                    preferred_element_type=_F32)
            upd = lax.dot_general(v_ref[rows, vc], kdec_ref[rows, kc], _TN, preferred_element_type=_F32)
            state_ref[hd] = st * dec_ref[c:c + 1, kc] + upd
            o = o * lax.rsqrt(jnp.mean(o * o, axis=-1, keepdims=True) + EPS)
            g = g_ref[rows, vc].astype(_F32)
            o_ref[rows, vc] = (o * gn_ref[:, vc] * (g * jax.nn.sigmoid(g))).astype(o_ref.dtype)


def _gla(proj, a_lr, w_gate_pad, b_gate, gla_norm, *, conv_width, dk, dv):
    s = proj.shape[0]
    h = GLA_HEADS
    tb = GLA_CHUNK * GLA_GROUP
    kw, vw = h * dk, h * dv
    q_blk = (3 * conv_width) // kw
    v_blk = (3 * conv_width + 2 * kw) // vw
    assert q_blk * kw == 3 * conv_width and v_blk * vw == 3 * conv_width + 2 * kw
    assert 2 * GLA_CHUNK == LANES and GLA_GROUP % 2 == 0 and s % tb == 0
    na = a_lr.shape[1]
    cum_op, pair_mask = _gla_constants(GLA_CHUNK, GLA_GROUP)
    body = functools.partial(_gla_body, dk=dk, dv=dv)
    return pl.pallas_call(
        body,
        out_shape=jax.ShapeDtypeStruct((s, vw), _BF16),
        grid=(s // tb,),
        in_specs=[pl.BlockSpec((tb, kw), lambda t: (t, q_blk)),
                  pl.BlockSpec((tb, kw), lambda t: (t, q_blk + 1)),
                  pl.BlockSpec((tb, vw), lambda t: (t, v_blk)),
                  pl.BlockSpec((tb, vw), lambda t: (t, v_blk + 1)),
                  pl.BlockSpec((tb, na), lambda t: (t, 0)),
                  pl.BlockSpec((na, kw), lambda t: (0, 0)),
                  pl.BlockSpec((1, kw), lambda t: (0, 0)),
                  pl.BlockSpec((1, vw), lambda t: (0, 0)),
                  pl.BlockSpec(cum_op.shape, lambda t: (0, 0)),
                  pl.BlockSpec(pair_mask.shape, lambda t: (0, 0, 0, 0))],
        out_specs=pl.BlockSpec((tb, vw), lambda t: (t, 0)),
        scratch_shapes=[pltpu.VMEM((h, dv, dk), _F32),
                        pltpu.VMEM((tb, vw), _F32),
                        pltpu.VMEM((tb, kw), _BF16),
                        pltpu.VMEM((tb, kw), _BF16),
                        pltpu.VMEM((SUBLANES, kw), _F32)],
        compiler_params=_params(("arbitrary",)),
        name="gla",
    )(proj, proj, proj, proj, a_lr, w_gate_pad, b_gate.reshape(1, -1).astype(_F32),
      gla_norm.reshape(1, -1).astype(_F32), cum_op, pair_mask)


def _xattn_body(q_ref, k_ref, v_ref, o_ref, *, dh):
    scale = dh ** -0.5
    for hd in range(XATTN_HEADS):
        sl = slice(hd * dh, (hd + 1) * dh)
        sc = lax.dot_general(q_ref[:, sl], k_ref[:, sl], _NT, preferred_element_type=_F32) * scale
        p = jnp.exp(sc - jnp.max(sc, axis=-1, keepdims=True))
        denom = jnp.sum(p, axis=-1, keepdims=True)
        pv = jnp.dot(p.astype(_BF16), v_ref[:, sl], preferred_element_type=_F32)
        o_ref[:, sl] = (pv / denom).astype(o_ref.dtype)


def _xattn(q, k, v, *, bm=1024):
    s, width = q.shape
    mlen = k.shape[0]
    body = functools.partial(_xattn_body, dh=width // XATTN_HEADS)
    return pl.pallas_call(
        body,
        out_shape=jax.ShapeDtypeStruct((s, width), _BF16),
        grid=(s // bm,),
        in_specs=[pl.BlockSpec((bm, width), lambda i: (i, 0)),
                  pl.BlockSpec((mlen, width), lambda i: (0, 0)),
                  pl.BlockSpec((mlen, width), lambda i: (0, 0))],
        out_specs=pl.BlockSpec((bm, width), lambda i: (i, 0)),
        compiler_params=_params(("parallel",)),
        name="xattn",
    )(q, k, v)


def kernel(x, mem, norm_mix, w_in, w_conv, w_gate_up, b_gate, gla_norm, w_out, norm_xattn, norm_mem,
           wq_x, wk_x, wv_x, wo_x, norm_mlp, w_up, w_down, norm_final):
    bsz, seq, d = x.shape
    depth = w_in.shape[0]
    conv_width = w_conv.shape[2]
    gla_key = w_gate_up.shape[2]
    gla_width = gla_norm.shape[1]
    gate_rank = w_gate_up.shape[1]
    dk = gla_key // GLA_HEADS
    dv = gla_width // GLA_HEADS
    main_cols = 3 * conv_width + 2 * gla_key + 2 * gla_width
    assert bsz == 1 and w_in.shape[2] == main_cols + gate_rank

    w_in_b, w_out_b = w_in.astype(_BF16), w_out.astype(_BF16)
    wq_b, wk_b, wv_b, wo_b = (w.astype(_BF16) for w in (wq_x, wk_x, wv_x, wo_x))
    w_up_b, w_down_b = w_up.astype(_BF16), w_down.astype(_BF16)
    w_a_all = jnp.zeros((depth, d, LANES), _BF16).at[:, :, :gate_rank].set(w_in[:, :, main_cols:].astype(_BF16))
    w_gate_all = jnp.zeros((depth, LANES, gla_key), _BF16).at[:, :gate_rank].set(w_gate_up.astype(_BF16))

    h = x.reshape(seq, d)
    mem2 = mem.reshape(mem.shape[1], d)
    for l in range(depth):
        xn, a_lr = _rmsnorm(h, norm_mix[l], _BF16, 512, w_a=w_a_all[l])
        proj = _matmul([xn], [(w_in_b, 0)], l, out_dtype=_BF16, name="in_proj")
        y_conv = _short_conv(proj, w_conv[l], width=conv_width)
        o_gla = _gla(proj, a_lr, w_gate_all[l], b_gate[l], gla_norm[l], conv_width=conv_width, dk=dk, dv=dv)
        h = _matmul([y_conv, o_gla], [(w_out_b, 0), (w_out_b, conv_width)], l, out_dtype=_F32, res=h,
                    bk=conv_width, name="out_proj")
        xn = _rmsnorm(h, norm_xattn[l], _BF16, 512)
        memn = _rmsnorm(mem2, norm_mem[l], _BF16, 256)
        q = _matmul([xn], [(wq_b, 0)], l, out_dtype=_BF16, name="xattn_q")
        k = _matmul([memn], [(wk_b, 0)], l, out_dtype=_BF16, name="xattn_k")
        v = _matmul([memn], [(wv_b, 0)], l, out_dtype=_BF16, name="xattn_v")
        att = _xattn(q, k, v)
        h = _matmul([att], [(wo_b, 0)], l, out_dtype=_F32, res=h, name="xattn_o")
        xn = _rmsnorm(h, norm_mlp[l], _BF16, 512)
        hid = _matmul([xn], [(w_up_b, 0)], l, out_dtype=_BF16, act="relu2", name="mlp_up")
        h = _matmul([hid], [(w_down_b, 0)], l, out_dtype=_F32, res=h, name="mlp_down")
    out = _rmsnorm(h, norm_final, _F32, 512)
    return out.reshape(bsz, seq, d)
```

```python
import functools
import math

import jax
import jax.numpy as jnp
import numpy as np
from jax import lax
from jax.experimental import pallas as pl
from jax.experimental.pallas import tpu as pltpu

EPS = 1e-6
GLA_CHUNK = 64
GLA_GROUP = 4
GLA_HEADS = 4
XATTN_HEADS = 4
GATE_TAU = 16.0
CONV_K = 3
LANES = 128
SUBLANES = 8
MIB = 1024 * 1024
VMEM_CAP = 60 * MIB
VMEM_TEMPS = 12 * MIB
VMEM_DEFAULT = 40 * MIB
LOG2E = math.log2(math.e)

_BF16 = jnp.bfloat16
_F32 = jnp.float32
_NT = (((1,), (1,)), ((), ()))
_TN = (((0,), (0,)), ((), ()))


def _params(sem, block_bytes=None):
    limit = VMEM_DEFAULT if block_bytes is None else min(VMEM_CAP, block_bytes + VMEM_TEMPS)
    return pltpu.CompilerParams(dimension_semantics=sem, vmem_limit_bytes=limit)


def _lane_partial_ssq(h):
    sq = h * h
    acc = sq[:, :LANES]
    for t in range(1, h.shape[1] // LANES):
        acc = acc + sq[:, t * LANES:(t + 1) * LANES]
    return acc


def _row_rsqrt(ssq_ref, inv_d):
    return lax.rsqrt(jnp.sum(ssq_ref[...], axis=-1, keepdims=True) * inv_d + EPS)


def _rmsnorm_body(x_ref, g_ref, o_ref):
    x = x_ref[...]
    y = x * lax.rsqrt(jnp.mean(x * x, axis=-1, keepdims=True) + EPS)
    o_ref[...] = (y * g_ref[...]).astype(o_ref.dtype)


def _rmsnorm(x, gain, out_dtype, bm):
    m, d = x.shape
    bm = min(bm, m)
    return pl.pallas_call(
        _rmsnorm_body,
        out_shape=jax.ShapeDtypeStruct((m, d), out_dtype),
        grid=(m // bm,),
        in_specs=[pl.BlockSpec((bm, d), lambda i: (i, 0)), pl.BlockSpec((1, d), lambda i: (0, 0))],
        out_specs=pl.BlockSpec((bm, d), lambda i: (i, 0)),
        compiler_params=_params(("parallel",)),
        name="rmsnorm",
    )(x, gain.reshape(1, d).astype(_F32))


def _norm_prep_body(x_ref, g_ref, hg_ref, ssq_ref):
    x = x_ref[...]
    hg_ref[...] = (x * g_ref[...]).astype(hg_ref.dtype)
    ssq_ref[...] = _lane_partial_ssq(x)


def _norm_prep(x, gain, bm=512):
    m, d = x.shape
    return pl.pallas_call(
        _norm_prep_body,
        out_shape=(jax.ShapeDtypeStruct((m, d), _BF16), jax.ShapeDtypeStruct((m, LANES), _F32)),
        grid=(m // bm,),
        in_specs=[pl.BlockSpec((bm, d), lambda i: (i, 0)), pl.BlockSpec((1, d), lambda i: (0, 0))],
        out_specs=(pl.BlockSpec((bm, d), lambda i: (i, 0)), pl.BlockSpec((bm, LANES), lambda i: (i, 0))),
        compiler_params=_params(("parallel",)),
        name="norm_prep",
    )(x, gain.reshape(1, d).astype(_F32))


def _mm_body(*refs, npairs, act, has_res, inv_d, emit_norm):
    it = iter(refs)
    xs = [next(it) for _ in range(npairs)]
    ws = [next(it) for _ in range(npairs)]
    res_ref = next(it) if has_res else None
    ssq_in_ref = next(it) if inv_d is not None else None
    gain_ref = next(it) if emit_norm else None
    o_ref = next(it)

    acc = jnp.dot(xs[0][...], ws[0][...], preferred_element_type=_F32)
    for p in range(1, npairs):
        acc = acc + jnp.dot(xs[p][...], ws[p][...], preferred_element_type=_F32)
    if inv_d is not None:
        acc = acc * _row_rsqrt(ssq_in_ref, inv_d)
    if act == "relu2":
        r = jnp.maximum(acc, 0.0)
        acc = r * r
    if has_res:
        acc = acc + res_ref[...]
    o_ref[...] = acc.astype(o_ref.dtype)

    if emit_norm:
        hg_ref, ssq_ref = next(it), next(it)
        hg_ref[...] = (acc * gain_ref[...]).astype(hg_ref.dtype)
        partial = _lane_partial_ssq(acc)
        j = pl.program_id(1)

        @pl.when(j == 0)
        def _():
            ssq_ref[...] = partial

        @pl.when(j > 0)
        def _():
            ssq_ref[...] += partial


def _matmul(xs, ws, layer, *, out_dtype, res=None, act=None, row_ssq=None, next_gain=None,
            bm=1024, bn=1024, name="matmul"):
    npairs = len(xs)
    m, kdim = xs[0].shape
    n = ws[0][0].shape[2]
    bm, bn = min(bm, m), min(bn, n)
    assert m % bm == 0
    grid = (m // bm, n // bn)
    in_specs = [pl.BlockSpec((bm, kdim), lambda i, j: (i, 0))] * npairs
    for _, row0 in ws:
        assert row0 % kdim == 0
        in_specs.append(pl.BlockSpec((None, kdim, bn), lambda i, j, kb=row0 // kdim: (layer, kb, j)))
    args = list(xs) + [w for w, _ in ws]
    out_bytes = jnp.dtype(out_dtype).itemsize
    block_bytes = 2 * (npairs * 2 * (bm * kdim + kdim * bn) + bm * bn * out_bytes)
    if res is not None:
        in_specs.append(pl.BlockSpec((bm, bn), lambda i, j: (i, j)))
        args.append(res)
        block_bytes += 2 * 4 * bm * bn
    inv_d = None
    if row_ssq is not None:
        in_specs.append(pl.BlockSpec((bm, LANES), lambda i, j: (i, 0)))
        args.append(row_ssq)
        inv_d = 1.0 / kdim
        block_bytes += 2 * 4 * bm * LANES
    out_shape = jax.ShapeDtypeStruct((m, grid[1] * bn), out_dtype)
    out_specs = pl.BlockSpec((bm, bn), lambda i, j: (i, j))
    if next_gain is not None:
        assert n % bn == 0
        in_specs.append(pl.BlockSpec((1, bn), lambda i, j: (0, j)))
        args.append(next_gain.reshape(1, n).astype(_F32))
        out_shape = (out_shape, jax.ShapeDtypeStruct((m, n), _BF16), jax.ShapeDtypeStruct((m, LANES), _F32))
        out_specs = (out_specs, pl.BlockSpec((bm, bn), lambda i, j: (i, j)),
                     pl.BlockSpec((bm, LANES), lambda i, j: (i, 0)))
        block_bytes += 2 * (2 * bm * bn + 4 * bm * LANES)
    body = functools.partial(_mm_body, npairs=npairs, act=act, has_res=res is not None, inv_d=inv_d,
                             emit_norm=next_gain is not None)
    return pl.pallas_call(
        body,
        out_shape=out_shape,
        grid=grid,
        in_specs=in_specs,
        out_specs=out_specs,
        compiler_params=_params(("parallel", "arbitrary"), block_bytes),
        name=name,
    )(*args)


def _conv_body(b_ref, c_ref, h_ref, w_ref, o_ref, tail_ref):
    t = pl.program_id(1)

    @pl.when(t == 0)
    def _():
        tail_ref[...] = jnp.zeros_like(tail_ref)

    u = c_ref[...].astype(_F32) * h_ref[...].astype(_F32)
    rows = u.shape[0]
    row = lax.broadcasted_iota(jnp.int32, u.shape, 0)
    prev1 = tail_ref[SUBLANES - 1:SUBLANES, :]
    prev2 = tail_ref[SUBLANES - 2:SUBLANES - 1, :]
    u1 = jnp.where(row == 0, prev1, pltpu.roll(u, 1, axis=0))
    u2 = jnp.where(row == 0, prev2, jnp.where(row == 1, prev1, pltpu.roll(u, 2, axis=0)))
    w = w_ref[...]
    conv = w[0:1, :] * u2 + w[1:2, :] * u1 + w[2:3, :] * u
    o_ref[...] = (b_ref[...].astype(_F32) * conv).astype(o_ref.dtype)
    tail_ref[...] = u[rows - SUBLANES:, :]


def _short_conv(proj, w_conv, *, width, tb=1024, cb=512):
    s = proj.shape[0]
    ncb = width // cb
    w_pad = jnp.zeros((SUBLANES, width), _F32).at[:CONV_K].set(w_conv.astype(_F32))
    return pl.pallas_call(
        _conv_body,
        out_shape=jax.ShapeDtypeStruct((s, width), _BF16),
        grid=(ncb, s // tb),
        in_specs=[pl.BlockSpec((tb, cb), lambda c, t: (t, c)),
                  pl.BlockSpec((tb, cb), lambda c, t: (t, ncb + c)),
                  pl.BlockSpec((tb, cb), lambda c, t: (t, 2 * ncb + c)),
                  pl.BlockSpec((SUBLANES, cb), lambda c, t: (0, c))],
        out_specs=pl.BlockSpec((tb, cb), lambda c, t: (t, c)),
        scratch_shapes=[pltpu.VMEM((SUBLANES, cb), _F32)],
        compiler_params=_params(("parallel", "arbitrary")),
        name="short_conv",
    )(proj, proj, proj, w_pad)


def _bf16_split_dot(a_exact_bf16, x):
    hi = x.astype(_BF16)
    lo = (x - hi.astype(_F32)).astype(_BF16)
    return (jnp.dot(a_exact_bf16, hi, preferred_element_type=_F32)
            + jnp.dot(a_exact_bf16, lo, preferred_element_type=_F32))


def _gla_constants(L, G):
    nlev = L.bit_length() - 1
    r = np.arange(L * G)
    cum_op = (r[None, :] <= r[:, None]) & (r[None, :] // L == r[:, None] // L)
    i = np.arange(L)
    base = [np.eye(L, dtype=bool)]
    for lev in range(nlev):
        s = L >> lev
        lower = (i % s) >= s // 2
        base.append((i[:, None] // s == i[None, :] // s) & lower[:, None] & ~lower[None, :])
    base = np.stack(base).astype(np.float32)
    masks = np.zeros((nlev + 1, 2, L, 2 * L), np.float32)
    masks[:, 0, :, :L] = base
    masks[:, 1, :, L:] = base
    return jnp.asarray(cum_op, _BF16), jnp.asarray(masks)


def _gla_level_operand(q, k, b, s, row8):
    half = s // 2
    n = b.shape[0]
    parts = []
    if half >= SUBLANES:
        for r0 in range(0, n, half):
            rows = slice(r0, r0 + half)
            r_bound = (r0 // s) * s + half - 1
            b_r = b[r_bound:r_bound + 1, :]
            if (r0 // half) % 2:
                parts.append(q[rows] * jnp.exp2(b[rows] - b_r))
            else:
                parts.append(k[rows] * jnp.exp2(b_r - b[rows]))
    else:
        low = (row8 & (s - 1)) >= half
        sign = jnp.where(low, -1.0, 1.0)
        for r0 in range(0, n, SUBLANES):
            rows = slice(r0, r0 + SUBLANES)
            b8 = b[rows]
            if half == 1:
                b_r = jnp.where(low, pltpu.roll(b8, 1, axis=0), b8)
            else:
                b_r = b8[half - 1:half, :]
                for blk in range(1, SUBLANES // s):
                    b_r = jnp.where(row8 >= blk * s, b8[blk * s + half - 1:blk * s + half, :], b_r)
            parts.append(jnp.where(low, q[rows], k[rows]) * jnp.exp2((b_r - b8) * sign))
    return jnp.concatenate(parts, axis=0).astype(_BF16)


def _gla_prepare_head(q, k, v, la, cum_op, mask_ref, row8):
    L, G = GLA_CHUNK, GLA_GROUP
    nlev = L.bit_length() - 1
    dk = q.shape[1]
    b = _bf16_split_dot(cum_op, la)
    b_last = [b[(c + 1) * L - 1:(c + 1) * L, :] for c in range(G)]
    q_in = (q * jnp.exp2(b)).astype(_BF16)
    k_dec = jnp.concatenate(
        [k[c * L:(c + 1) * L] * jnp.exp2(b_last[c] - b[c * L:(c + 1) * L]) for c in range(G)], axis=0
    ).astype(_BF16)
    decays = [jnp.exp2(bl) for bl in b_last]

    acc = [None] * G

    def add_level(idx, p):
        for c in range(G):
            t = c // 2
            blk = p[c * L:(c + 1) * L, t * LANES:(t + 1) * LANES] * mask_ref[idx, c % 2]
            acc[c] = blk if acc[c] is None else acc[c] + blk

    add_level(0, lax.dot_general(q.astype(_BF16), k.astype(_BF16), _NT, preferred_element_type=_F32))
    for lev in range(nlev):
        x = _gla_level_operand(q, k, b, L >> lev, row8)
        add_level(lev + 1, lax.dot_general(x, x, _NT, preferred_element_type=_F32))

    zeros = jnp.zeros((L, LANES), _F32)
    a_bd = jnp.concatenate(
        [jnp.concatenate([acc[c] if t == c // 2 else zeros for t in range(G // 2)], axis=1) for c in range(G)],
        axis=0).astype(_BF16)
    o_intra = jnp.dot(a_bd, v, preferred_element_type=_F32)
    return o_intra, q_in, k_dec, decays


def _gla_body(q_ref, k_ref, v_ref, g_ref, a_ref, wg_ref, bg_ref, gn_ref, cum_ref, mask_ref, o_ref,
              state_ref, oin_ref, qin_ref, kdec_ref, dec_ref, *, dk, dv):
    L, G = GLA_CHUNK, GLA_GROUP

    @pl.when(pl.program_id(0) == 0)
    def _():
        state_ref[...] = jnp.zeros_like(state_ref)

    z = jnp.dot(a_ref[...].astype(_BF16), wg_ref[...], preferred_element_type=_F32) + bg_ref[...]
    la_all = (jnp.minimum(z, 0.0) - jnp.log(1.0 + jnp.exp(-jnp.abs(z)))) * (LOG2E / GATE_TAU)

    row8 = lax.broadcasted_iota(jnp.int32, (SUBLANES, dk), 0)
    scale = dk ** -0.5
    cum_op = cum_ref[...]
    for hd in range(GLA_HEADS):
        kc = slice(hd * dk, (hd + 1) * dk)
        vc = slice(hd * dv, (hd + 1) * dv)
        o_intra, q_in, k_dec, decays = _gla_prepare_head(
            q_ref[:, kc].astype(_F32) * scale, k_ref[:, kc].astype(_F32), v_ref[:, vc], la_all[:, kc],
            cum_op, mask_ref, row8)
        oin_ref[:, vc] = o_intra
        qin_ref[:, kc] = q_in
        kdec_ref[:, kc] = k_dec
        for c in range(G):
            dec_ref[c:c + 1, kc] = decays[c]

    for c in range(G):
        rows = slice(c * L, (c + 1) * L)
        for hd in range(GLA_HEADS):
            kc = slice(hd * dk, (hd + 1) * dk)
            vc = slice(hd * dv, (hd + 1) * dv)
            st = state_ref[hd]
            o = oin_ref[rows, vc] + lax.dot_general(qin_ref[rows, kc], st.astype(_BF16), _NT,
                                                    preferred_element_type=_F32)
            upd = lax.dot_general(v_ref[rows, vc], kdec_ref[rows, kc], _TN, preferred_element_type=_F32)
            state_ref[hd] = st * dec_ref[c:c + 1, kc] + upd
            o = o * lax.rsqrt(jnp.mean(o * o, axis=-1, keepdims=True) + EPS)
            g = g_ref[rows, vc].astype(_F32)
            o_ref[rows, vc] = (o * gn_ref[:, vc] * (g * jax.nn.sigmoid(g))).astype(o_ref.dtype)


def _gla(proj, a_lr, w_gate_pad, b_gate, gla_norm, *, conv_width, dk, dv):
    s = proj.shape[0]
    h = GLA_HEADS
    tb = GLA_CHUNK * GLA_GROUP
    kw, vw = h * dk, h * dv
    q_blk = (3 * conv_width) // kw
    v_blk = (3 * conv_width + 2 * kw) // vw
    assert q_blk * kw == 3 * conv_width and v_blk * vw == 3 * conv_width + 2 * kw
    assert 2 * GLA_CHUNK == LANES and GLA_GROUP % 2 == 0 and s % tb == 0
    na = a_lr.shape[1]
    cum_op, pair_mask = _gla_constants(GLA_CHUNK, GLA_GROUP)
    body = functools.partial(_gla_body, dk=dk, dv=dv)
    return pl.pallas_call(
        body,
        out_shape=jax.ShapeDtypeStruct((s, vw), _BF16),
        grid=(s // tb,),
        in_specs=[pl.BlockSpec((tb, kw), lambda t: (t, q_blk)),
                  pl.BlockSpec((tb, kw), lambda t: (t, q_blk + 1)),
                  pl.BlockSpec((tb, vw), lambda t: (t, v_blk)),
                  pl.BlockSpec((tb, vw), lambda t: (t, v_blk + 1)),
                  pl.BlockSpec((tb, na), lambda t: (t, 0)),
                  pl.BlockSpec((na, kw), lambda t: (0, 0)),
                  pl.BlockSpec((1, kw), lambda t: (0, 0)),
                  pl.BlockSpec((1, vw), lambda t: (0, 0)),
                  pl.BlockSpec(cum_op.shape, lambda t: (0, 0)),
                  pl.BlockSpec(pair_mask.shape, lambda t: (0, 0, 0, 0))],
        out_specs=pl.BlockSpec((tb, vw), lambda t: (t, 0)),
        scratch_shapes=[pltpu.VMEM((h, dv, dk), _F32),
                        pltpu.VMEM((tb, vw), _F32),
                        pltpu.VMEM((tb, kw), _BF16),
                        pltpu.VMEM((tb, kw), _BF16),
                        pltpu.VMEM((SUBLANES, kw), _F32)],
        compiler_params=_params(("arbitrary",)),
        name="gla",
    )(proj, proj, proj, proj, a_lr, w_gate_pad, b_gate.reshape(1, -1).astype(_F32),
      gla_norm.reshape(1, -1).astype(_F32), cum_op, pair_mask)


def _xattn_body(q_ref, k_ref, v_ref, o_ref, *, dh):
    scale = dh ** -0.5
    for hd in range(XATTN_HEADS):
        sl = slice(hd * dh, (hd + 1) * dh)
        sc = lax.dot_general(q_ref[:, sl], k_ref[:, sl], _NT, preferred_element_type=_F32) * scale
        p = jnp.exp(sc - jnp.max(sc, axis=-1, keepdims=True))
        denom = jnp.sum(p, axis=-1, keepdims=True)
        pv = jnp.dot(p.astype(_BF16), v_ref[:, sl], preferred_element_type=_F32)
        o_ref[:, sl] = (pv / denom).astype(o_ref.dtype)


def _xattn(q, k, v, *, bm=1024):
    s, width = q.shape
    mlen = k.shape[0]
    body = functools.partial(_xattn_body, dh=width // XATTN_HEADS)
    return pl.pallas_call(
        body,
        out_shape=jax.ShapeDtypeStruct((s, width), _BF16),
        grid=(s // bm,),
        in_specs=[pl.BlockSpec((bm, width), lambda i: (i, 0)),
                  pl.BlockSpec((mlen, width), lambda i: (0, 0)),
                  pl.BlockSpec((mlen, width), lambda i: (0, 0))],
        out_specs=pl.BlockSpec((bm, width), lambda i: (i, 0)),
        compiler_params=_params(("parallel",)),
        name="xattn",
    )(q, k, v)


def kernel(x, mem, norm_mix, w_in, w_conv, w_gate_up, b_gate, gla_norm, w_out, norm_xattn, norm_mem,
           wq_x, wk_x, wv_x, wo_x, norm_mlp, w_up, w_down, norm_final):
    bsz, seq, d = x.shape
    depth = w_in.shape[0]
    conv_width = w_conv.shape[2]
    gla_key = w_gate_up.shape[2]
    gla_width = gla_norm.shape[1]
    gate_rank = w_gate_up.shape[1]
    dk = gla_key // GLA_HEADS
    dv = gla_width // GLA_HEADS
    main_cols = 3 * conv_width + 2 * gla_key + 2 * gla_width
    assert bsz == 1 and w_in.shape[2] == main_cols + gate_rank

    w_in_b, w_out_b = w_in[:, :, :main_cols].astype(_BF16), w_out.astype(_BF16)
    wq_b, wk_b, wv_b, wo_b = (w.astype(_BF16) for w in (wq_x, wk_x, wv_x, wo_x))
    w_up_b, w_down_b = w_up.astype(_BF16), w_down.astype(_BF16)
    w_a_all = jnp.zeros((depth, d, LANES), _BF16).at[:, :, :gate_rank].set(w_in[:, :, main_cols:].astype(_BF16))
    w_gate_all = jnp.zeros((depth, LANES, gla_key), _BF16).at[:, :gate_rank].set(w_gate_up.astype(_BF16))

    h = x.reshape(seq, d)
    mem2 = mem.reshape(mem.shape[1], d)
    hg, ssq = _norm_prep(h, norm_mix[0])
    for l in range(depth):
        proj = _matmul([hg], [(w_in_b, 0)], l, out_dtype=_BF16, row_ssq=ssq, name="in_proj")
        a_lr = _matmul([hg], [(w_a_all, 0)], l, out_dtype=_F32, row_ssq=ssq, name="gate_proj")
        y_conv = _short_conv(proj, w_conv[l], width=conv_width)
        o_gla = _gla(proj, a_lr, w_gate_all[l], b_gate[l], gla_norm[l], conv_width=conv_width, dk=dk, dv=dv)
        h, hg, ssq = _matmul([y_conv, o_gla], [(w_out_b, 0), (w_out_b, conv_width)], l, out_dtype=_F32, res=h,
                             next_gain=norm_xattn[l], bn=512, name="out_proj")
        memn = _rmsnorm(mem2, norm_mem[l], _BF16, 256)
        q = _matmul([hg], [(wq_b, 0)], l, out_dtype=_BF16, row_ssq=ssq, name="xattn_q")
        k = _matmul([memn], [(wk_b, 0)], l, out_dtype=_BF16, name="xattn_k")
        v = _matmul([memn], [(wv_b, 0)], l, out_dtype=_BF16, name="xattn_v")
        att = _xattn(q, k, v)
        h, hg, ssq = _matmul([att], [(wo_b, 0)], l, out_dtype=_F32, res=h, next_gain=norm_mlp[l], name="xattn_o")
        hid = _matmul([hg], [(w_up_b, 0)], l, out_dtype=_BF16, act="relu2", row_ssq=ssq, name="mlp_up")
        if l + 1 < depth:
            h, hg, ssq = _matmul([hid], [(w_down_b, 0)], l, out_dtype=_F32, res=h, next_gain=norm_mix[l + 1],
                                 bm=512, bn=256, name="mlp_down")
        else:
            h = _matmul([hid], [(w_down_b, 0)], l, out_dtype=_F32, res=h, bm=512, bn=256, name="mlp_down")
    out = _rmsnorm(h, norm_final, _F32, 512)
    return out.reshape(bsz, seq, d)
```

```python
import functools
import math

import jax
import jax.numpy as jnp
import numpy as np
from jax import lax
from jax.experimental import pallas as pl
from jax.experimental.pallas import tpu as pltpu

EPS = 1e-6
GLA_CHUNK = 64
GLA_GROUP = 4
GLA_HEADS = 4
XATTN_HEADS = 4
GATE_TAU = 16.0
CONV_K = 3
LANES = 128
SUBLANES = 8
MIB = 1024 * 1024
VMEM_CAP = 60 * MIB
VMEM_TEMPS = 12 * MIB
VMEM_DEFAULT = 40 * MIB
LOG2E = math.log2(math.e)

_BF16 = jnp.bfloat16
_F32 = jnp.float32
_NT = (((1,), (1,)), ((), ()))
_TN = (((0,), (0,)), ((), ()))


def _params(sem, block_bytes=None):
    limit = VMEM_DEFAULT if block_bytes is None else min(VMEM_CAP, block_bytes + VMEM_TEMPS)
    return pltpu.CompilerParams(dimension_semantics=sem, vmem_limit_bytes=limit)


def _lane_partial_ssq(h):
    sq = h * h
    acc = sq[:, :LANES]
    for t in range(1, h.shape[1] // LANES):
        acc = acc + sq[:, t * LANES:(t + 1) * LANES]
    return acc


def _row_rsqrt(ssq_ref, inv_d):
    return lax.rsqrt(jnp.sum(ssq_ref[...], axis=-1, keepdims=True) * inv_d + EPS)


def _rmsnorm_body(x_ref, g_ref, o_ref):
    x = x_ref[...]
    y = x * lax.rsqrt(jnp.mean(x * x, axis=-1, keepdims=True) + EPS)
    o_ref[...] = (y * g_ref[...]).astype(o_ref.dtype)


def _rmsnorm(x, gain, out_dtype, bm):
    m, d = x.shape
    bm = min(bm, m)
    return pl.pallas_call(
        _rmsnorm_body,
        out_shape=jax.ShapeDtypeStruct((m, d), out_dtype),
        grid=(m // bm,),
        in_specs=[pl.BlockSpec((bm, d), lambda i: (i, 0)), pl.BlockSpec((1, d), lambda i: (0, 0))],
        out_specs=pl.BlockSpec((bm, d), lambda i: (i, 0)),
        compiler_params=_params(("parallel",)),
        name="rmsnorm",
    )(x, gain.reshape(1, d).astype(_F32))


def _norm_prep_body(x_ref, g_ref, wa_ref, hg_ref, ssq_ref, a_ref, *, inv_d):
    x = x_ref[...]
    hg = (x * g_ref[...]).astype(hg_ref.dtype)
    hg_ref[...] = hg
    ssq = _lane_partial_ssq(x)
    ssq_ref[...] = ssq
    r = lax.rsqrt(jnp.sum(ssq, axis=-1, keepdims=True) * inv_d + EPS)
    a_ref[...] = jnp.dot(hg, wa_ref[...], preferred_element_type=_F32) * r


def _norm_prep(x, gain, w_a, layer, bm=512):
    m, d = x.shape
    na = w_a.shape[2]
    row_block = lambda width: pl.BlockSpec((bm, width), lambda i: (i, 0))
    return pl.pallas_call(
        functools.partial(_norm_prep_body, inv_d=1.0 / d),
        out_shape=(jax.ShapeDtypeStruct((m, d), _BF16), jax.ShapeDtypeStruct((m, LANES), _F32),
                   jax.ShapeDtypeStruct((m, na), _F32)),
        grid=(m // bm,),
        in_specs=[row_block(d), pl.BlockSpec((1, d), lambda i: (0, 0)),
                  pl.BlockSpec((None, d, na), lambda i: (layer, 0, 0))],
        out_specs=(row_block(d), row_block(LANES), row_block(na)),
        compiler_params=_params(("parallel",)),
        name="norm_prep",
    )(x, gain.reshape(1, d).astype(_F32), w_a)


def _mm_body(*refs, npairs, nk, act, has_res, inv_d, emit_norm, nside):
    it = iter(refs)
    xs = [next(it) for _ in range(npairs)]
    ws = [next(it) for _ in range(npairs)]
    res_ref = next(it) if has_res else None
    ssq_in_ref = next(it) if inv_d is not None else None
    gain_ref = next(it) if emit_norm else None
    side_src = [next(it) for _ in range(nside)]
    o_ref = next(it)
    hg_ref, ssq_ref = (next(it), next(it)) if emit_norm else (None, None)
    side_dst = [next(it) for _ in range(nside)]

    for src, dst in zip(side_src, side_dst):
        dst[...] = src[...].astype(dst.dtype)

    acc = jnp.dot(xs[0][...], ws[0][...], preferred_element_type=_F32)
    for p in range(1, npairs):
        acc = acc + jnp.dot(xs[p][...], ws[p][...], preferred_element_type=_F32)

    if nk > 1:
        @pl.when(pl.program_id(2) == 0)
        def _():
            o_ref[...] = res_ref[...] if has_res else jnp.zeros_like(o_ref)

        o_ref[...] += acc
        return

    if inv_d is not None:
        acc = acc * _row_rsqrt(ssq_in_ref, inv_d)
    if act == "relu2":
        r = jnp.maximum(acc, 0.0)
        acc = r * r
    if has_res:
        acc = acc + res_ref[...]
    o_ref[...] = acc.astype(o_ref.dtype)

    if emit_norm:
        hg_ref[...] = (acc * gain_ref[...]).astype(hg_ref.dtype)
        partial = _lane_partial_ssq(acc)
        j = pl.program_id(1)

        @pl.when(j == 0)
        def _():
            ssq_ref[...] = partial

        @pl.when(j > 0)
        def _():
            ssq_ref[...] += partial


def _matmul(xs, ws, *, out_dtype, res=None, act=None, row_ssq=None, next_gain=None, side_casts=(),
            bm=1024, bn=1024, bk=4096, name="matmul"):
    npairs = len(xs)
    m, kdim = xs[0].shape
    n = ws[0][0].shape[-1]
    bm, bn, bk = min(bm, m), min(bn, n), min(bk, kdim)
    nk = kdim // bk
    assert m % bm == 0 and kdim % bk == 0
    assert nk == 1 or (act is None and row_ssq is None and next_gain is None and out_dtype == _F32)
    grid = (m // bm, n // bn, nk)
    nsteps = grid[0] * grid[1] * nk
    in_specs = [pl.BlockSpec((bm, bk), lambda i, j, k: (i, k))] * npairs
    for w, layer, row0 in ws:
        assert row0 % bk == 0
        kb = row0 // bk
        if layer is None:
            in_specs.append(pl.BlockSpec((bk, bn), lambda i, j, k, kb=kb: (kb + k, j)))
        else:
            in_specs.append(pl.BlockSpec((None, bk, bn), lambda i, j, k, kb=kb, layer=layer: (layer, kb + k, j)))
    args = list(xs) + [w for w, _, _ in ws]
    out_bytes = jnp.dtype(out_dtype).itemsize
    block_bytes = 2 * (npairs * 2 * (bm * bk + bk * bn) + bm * bn * out_bytes)
    if res is not None:
        in_specs.append(pl.BlockSpec((bm, bn), lambda i, j, k: (i, j)))
        args.append(res)
        block_bytes += 2 * 4 * bm * bn
    inv_d = None
    if row_ssq is not None:
        in_specs.append(pl.BlockSpec((bm, LANES), lambda i, j, k: (i, 0)))
        args.append(row_ssq)
        inv_d = 1.0 / kdim
        block_bytes += 2 * 4 * bm * LANES
    out_shape = [jax.ShapeDtypeStruct((m, grid[1] * bn), out_dtype)]
    out_specs = [pl.BlockSpec((bm, bn), lambda i, j, k: (i, j))]
    if next_gain is not None:
        assert n % bn == 0
        in_specs.append(pl.BlockSpec((1, bn), lambda i, j, k: (0, j)))
        args.append(next_gain.reshape(1, n).astype(_F32))
        out_shape += [jax.ShapeDtypeStruct((m, n), _BF16), jax.ShapeDtypeStruct((m, LANES), _F32)]
        out_specs += [pl.BlockSpec((bm, bn), lambda i, j, k: (i, j)),
                      pl.BlockSpec((bm, LANES), lambda i, j, k: (i, 0))]
        block_bytes += 2 * (2 * bm * bn + 4 * bm * LANES)
    for src, layer in side_casts:
        _, rows, cols = src.shape
        slab = rows // nsteps
        assert slab * nsteps == rows and slab % (2 * SUBLANES) == 0
        step = lambda i, j, k: (i * grid[1] + j) * nk + k
        in_specs.append(pl.BlockSpec((None, slab, cols), lambda i, j, k, layer=layer: (layer, step(i, j, k), 0)))
        args.append(src)
        out_shape.append(jax.ShapeDtypeStruct((rows, cols), _BF16))
        out_specs.append(pl.BlockSpec((slab, cols), lambda i, j, k: (step(i, j, k), 0)))
        block_bytes += 2 * (4 + 2) * slab * cols
    body = functools.partial(_mm_body, npairs=npairs, nk=nk, act=act, has_res=res is not None, inv_d=inv_d,
                             emit_norm=next_gain is not None, nside=len(side_casts))
    outs = pl.pallas_call(
        body,
        out_shape=tuple(out_shape),
        grid=grid,
        in_specs=in_specs,
        out_specs=tuple(out_specs),
        compiler_params=_params(("parallel", "arbitrary", "arbitrary"), block_bytes),
        name=name,
    )(*args)
    return outs[0] if len(outs) == 1 else outs


def _conv_body(b_ref, c_ref, h_ref, w_ref, o_ref, tail_ref):
    t = pl.program_id(1)

    @pl.when(t == 0)
    def _():
        tail_ref[...] = jnp.zeros_like(tail_ref)

    u = c_ref[...].astype(_F32) * h_ref[...].astype(_F32)
    rows = u.shape[0]
    row = lax.broadcasted_iota(jnp.int32, u.shape, 0)
    prev1 = tail_ref[SUBLANES - 1:SUBLANES, :]
    prev2 = tail_ref[SUBLANES - 2:SUBLANES - 1, :]
    u1 = jnp.where(row == 0, prev1, pltpu.roll(u, 1, axis=0))
    u2 = jnp.where(row == 0, prev2, jnp.where(row == 1, prev1, pltpu.roll(u, 2, axis=0)))
    w = w_ref[...]
    conv = w[0:1, :] * u2 + w[1:2, :] * u1 + w[2:3, :] * u
    o_ref[...] = (b_ref[...].astype(_F32) * conv).astype(o_ref.dtype)
    tail_ref[...] = u[rows - SUBLANES:, :]


def _short_conv(proj, w_conv, *, width, tb=1024, cb=512):
    s = proj.shape[0]
    ncb = width // cb
    w_pad = jnp.zeros((SUBLANES, width), _F32).at[:CONV_K].set(w_conv.astype(_F32))
    return pl.pallas_call(
        _conv_body,
        out_shape=jax.ShapeDtypeStruct((s, width), _BF16),
        grid=(ncb, s // tb),
        in_specs=[pl.BlockSpec((tb, cb), lambda c, t: (t, c)),
                  pl.BlockSpec((tb, cb), lambda c, t: (t, ncb + c)),
                  pl.BlockSpec((tb, cb), lambda c, t: (t, 2 * ncb + c)),
                  pl.BlockSpec((SUBLANES, cb), lambda c, t: (0, c))],
        out_specs=pl.BlockSpec((tb, cb), lambda c, t: (t, c)),
        scratch_shapes=[pltpu.VMEM((SUBLANES, cb), _F32)],
        compiler_params=_params(("parallel", "arbitrary")),
        name="short_conv",
    )(proj, proj, proj, w_pad)


def _bf16_split_dot(a_exact_bf16, x):
    hi = x.astype(_BF16)
    lo = (x - hi.astype(_F32)).astype(_BF16)
    return (jnp.dot(a_exact_bf16, hi, preferred_element_type=_F32)
            + jnp.dot(a_exact_bf16, lo, preferred_element_type=_F32))


def _gla_constants(L, G):
    nlev = L.bit_length() - 1
    r = np.arange(L * G)
    cum_op = (r[None, :] <= r[:, None]) & (r[None, :] // L == r[:, None] // L)
    i = np.arange(L)
    base = [np.eye(L, dtype=bool)]
    for lev in range(nlev):
        s = L >> lev
        lower = (i % s) >= s // 2
        base.append((i[:, None] // s == i[None, :] // s) & lower[:, None] & ~lower[None, :])
    base = np.stack(base).astype(np.float32)
    masks = np.zeros((nlev + 1, 2, L, 2 * L), np.float32)
    masks[:, 0, :, :L] = base
    masks[:, 1, :, L:] = base
    return jnp.asarray(cum_op, _BF16), jnp.asarray(masks)


def _gla_level_operand(q, k, b, s, row8):
    half = s // 2
    n = b.shape[0]
    parts = []
    if half >= SUBLANES:
        for r0 in range(0, n, half):
            rows = slice(r0, r0 + half)
            r_bound = (r0 // s) * s + half - 1
            b_r = b[r_bound:r_bound + 1, :]
            if (r0 // half) % 2:
                parts.append(q[rows] * jnp.exp2(b[rows] - b_r))
            else:
                parts.append(k[rows] * jnp.exp2(b_r - b[rows]))
    else:
        low = (row8 & (s - 1)) >= half
        sign = jnp.where(low, -1.0, 1.0)
        for r0 in range(0, n, SUBLANES):
            rows = slice(r0, r0 + SUBLANES)
            b8 = b[rows]
            if half == 1:
                b_r = jnp.where(low, pltpu.roll(b8, 1, axis=0), b8)
            else:
                b_r = b8[half - 1:half, :]
                for blk in range(1, SUBLANES // s):
                    b_r = jnp.where(row8 >= blk * s, b8[blk * s + half - 1:blk * s + half, :], b_r)
            parts.append(jnp.where(low, q[rows], k[rows]) * jnp.exp2((b_r - b8) * sign))
    return jnp.concatenate(parts, axis=0).astype(_BF16)


def _gla_prepare_head(q, k, v, la, cum_op, mask_ref, row8):
    L, G = GLA_CHUNK, GLA_GROUP
    nlev = L.bit_length() - 1
    dk = q.shape[1]
    b = _bf16_split_dot(cum_op, la)
    b_last = [b[(c + 1) * L - 1:(c + 1) * L, :] for c in range(G)]
    q_in = (q * jnp.exp2(b)).astype(_BF16)
    k_dec = jnp.concatenate(
        [k[c * L:(c + 1) * L] * jnp.exp2(b_last[c] - b[c * L:(c + 1) * L]) for c in range(G)], axis=0
    ).astype(_BF16)
    decays = [jnp.exp2(bl) for bl in b_last]

    acc = [None] * G

    def add_level(idx, p):
        for c in range(G):
            t = c // 2
            blk = p[c * L:(c + 1) * L, t * LANES:(t + 1) * LANES] * mask_ref[idx, c % 2]
            acc[c] = blk if acc[c] is None else acc[c] + blk

    add_level(0, lax.dot_general(q.astype(_BF16), k.astype(_BF16), _NT, preferred_element_type=_F32))
    for lev in range(nlev):
        x = _gla_level_operand(q, k, b, L >> lev, row8)
        add_level(lev + 1, lax.dot_general(x, x, _NT, preferred_element_type=_F32))

    zeros = jnp.zeros((L, LANES), _F32)
    a_bd = jnp.concatenate(
        [jnp.concatenate([acc[c] if t == c // 2 else zeros for t in range(G // 2)], axis=1) for c in range(G)],
        axis=0).astype(_BF16)
    o_intra = jnp.dot(a_bd, v, preferred_element_type=_F32)
    return o_intra, q_in, k_dec, decays


def _gla_body(q_ref, k_ref, v_ref, g_ref, a_ref, wg_ref, bg_ref, gn_ref, cum_ref, mask_ref, o_ref,
              state_ref, oin_ref, qin_ref, kdec_ref, dec_ref, *, dk, dv):
    L, G = GLA_CHUNK, GLA_GROUP

    @pl.when(pl.program_id(0) == 0)
    def _():
        state_ref[...] = jnp.zeros_like(state_ref)

    z = jnp.dot(a_ref[...].astype(_BF16), wg_ref[...], preferred_element_type=_F32) + bg_ref[...]
    la_all = (jnp.minimum(z, 0.0) - jnp.log(1.0 + jnp.exp(-jnp.abs(z)))) * (LOG2E / GATE_TAU)

    row8 = lax.broadcasted_iota(jnp.int32, (SUBLANES, dk), 0)
    scale = dk ** -0.5
    cum_op = cum_ref[...]
    for hd in range(GLA_HEADS):
        kc = slice(hd * dk, (hd + 1) * dk)
        vc = slice(hd * dv, (hd + 1) * dv)
        o_intra, q_in, k_dec, decays = _gla_prepare_head(
            q_ref[:, kc].astype(_F32) * scale, k_ref[:, kc].astype(_F32), v_ref[:, vc], la_all[:, kc],
            cum_op, mask_ref, row8)
        oin_ref[:, vc] = o_intra
        qin_ref[:, kc] = q_in
        kdec_ref[:, kc] = k_dec
        for c in range(G):
            dec_ref[c:c + 1, kc] = decays[c]

    for c in range(G):
        rows = slice(c * L, (c + 1) * L)
        for hd in range(GLA_HEADS):
            kc = slice(hd * dk, (hd + 1) * dk)
            vc = slice(hd * dv, (hd + 1) * dv)
            st = state_ref[hd]
            o = oin_ref[rows, vc] + lax.dot_general(qin_ref[rows, kc], st.astype(_BF16), _NT,
                                                    preferred_element_type=_F32)
            upd = lax.dot_general(v_ref[rows, vc], kdec_ref[rows, kc], _TN, preferred_element_type=_F32)
            state_ref[hd] = st * dec_ref[c:c + 1, kc] + upd
            o = o * lax.rsqrt(jnp.mean(o * o, axis=-1, keepdims=True) + EPS)
            g = g_ref[rows, vc].astype(_F32)
            o_ref[rows, vc] = (o * gn_ref[:, vc] * (g * jax.nn.sigmoid(g))).astype(o_ref.dtype)


def _gla(proj, a_lr, w_gate_pad, b_gate, gla_norm, *, conv_width, dk, dv):
    s = proj.shape[0]
    h = GLA_HEADS
    tb = GLA_CHUNK * GLA_GROUP
    kw, vw = h * dk, h * dv
    q_blk = (3 * conv_width) // kw
    v_blk = (3 * conv_width + 2 * kw) // vw
    assert q_blk * kw == 3 * conv_width and v_blk * vw == 3 * conv_width + 2 * kw
    assert 2 * GLA_CHUNK == LANES and GLA_GROUP % 2 == 0 and s % tb == 0
    na = a_lr.shape[1]
    cum_op, pair_mask = _gla_constants(GLA_CHUNK, GLA_GROUP)
    body = functools.partial(_gla_body, dk=dk, dv=dv)
    return pl.pallas_call(
        body,
        out_shape=jax.ShapeDtypeStruct((s, vw), _BF16),
        grid=(s // tb,),
        in_specs=[pl.BlockSpec((tb, kw), lambda t: (t, q_blk)),
                  pl.BlockSpec((tb, kw), lambda t: (t, q_blk + 1)),
                  pl.BlockSpec((tb, vw), lambda t: (t, v_blk)),
                  pl.BlockSpec((tb, vw), lambda t: (t, v_blk + 1)),
                  pl.BlockSpec((tb, na), lambda t: (t, 0)),
                  pl.BlockSpec((na, kw), lambda t: (0, 0)),
                  pl.BlockSpec((1, kw), lambda t: (0, 0)),
                  pl.BlockSpec((1, vw), lambda t: (0, 0)),
                  pl.BlockSpec(cum_op.shape, lambda t: (0, 0)),
                  pl.BlockSpec(pair_mask.shape, lambda t: (0, 0, 0, 0))],
        out_specs=pl.BlockSpec((tb, vw), lambda t: (t, 0)),
        scratch_shapes=[pltpu.VMEM((h, dv, dk), _F32),
                        pltpu.VMEM((tb, vw), _F32),
                        pltpu.VMEM((tb, kw), _BF16),
                        pltpu.VMEM((tb, kw), _BF16),
                        pltpu.VMEM((SUBLANES, kw), _F32)],
        compiler_params=_params(("arbitrary",)),
        name="gla",
    )(proj, proj, proj, proj, a_lr, w_gate_pad, b_gate.reshape(1, -1).astype(_F32),
      gla_norm.reshape(1, -1).astype(_F32), cum_op, pair_mask)


def _xattn_body(q_ref, k_ref, v_ref, o_ref, *, dh):
    scale = dh ** -0.5
    for hd in range(XATTN_HEADS):
        sl = slice(hd * dh, (hd + 1) * dh)
        sc = lax.dot_general(q_ref[:, sl], k_ref[:, sl], _NT, preferred_element_type=_F32) * scale
        p = jnp.exp(sc - jnp.max(sc, axis=-1, keepdims=True))
        denom = jnp.sum(p, axis=-1, keepdims=True)
        pv = jnp.dot(p.astype(_BF16), v_ref[:, sl], preferred_element_type=_F32)
        o_ref[:, sl] = (pv / denom).astype(o_ref.dtype)


def _xattn(q, k, v, *, bm=1024):
    s, width = q.shape
    mlen = k.shape[0]
    body = functools.partial(_xattn_body, dh=width // XATTN_HEADS)
    return pl.pallas_call(
        body,
        out_shape=jax.ShapeDtypeStruct((s, width), _BF16),
        grid=(s // bm,),
        in_specs=[pl.BlockSpec((bm, width), lambda i: (i, 0)),
                  pl.BlockSpec((mlen, width), lambda i: (0, 0)),
                  pl.BlockSpec((mlen, width), lambda i: (0, 0))],
        out_specs=pl.BlockSpec((bm, width), lambda i: (i, 0)),
        compiler_params=_params(("parallel",)),
        name="xattn",
    )(q, k, v)


def kernel(x, mem, norm_mix, w_in, w_conv, w_gate_up, b_gate, gla_norm, w_out, norm_xattn, norm_mem,
           wq_x, wk_x, wv_x, wo_x, norm_mlp, w_up, w_down, norm_final):
    bsz, seq, d = x.shape
    depth = w_in.shape[0]
    conv_width = w_conv.shape[2]
    gla_key = w_gate_up.shape[2]
    gla_width = gla_norm.shape[1]
    gate_rank = w_gate_up.shape[1]
    dk = gla_key // GLA_HEADS
    dv = gla_width // GLA_HEADS
    main_cols = 3 * conv_width + 2 * gla_key + 2 * gla_width
    assert bsz == 1 and w_in.shape[2] == main_cols + gate_rank

    wq_b, wk_b, wv_b, wo_b = (w.astype(_BF16) for w in (wq_x, wk_x, wv_x, wo_x))
    w_in_b = w_in.astype(_BF16)
    w_a_all = jnp.zeros((depth, d, LANES), _BF16).at[:, :, :gate_rank].set(w_in_b[:, :, main_cols:])
    w_gate_all = jnp.zeros((depth, LANES, gla_key), _BF16).at[:, :gate_rank].set(w_gate_up.astype(_BF16))

    h = x.reshape(seq, d)
    mem2 = mem.reshape(mem.shape[1], d)
    for l in range(depth):
        hg, ssq, a_lr = _norm_prep(h, norm_mix[l], w_a_all, l)
        proj, w_up_l, w_out_l = _matmul([hg], [(w_in_b, l, 0)], out_dtype=_BF16, row_ssq=ssq, bn=768,
                                        side_casts=[(w_up, l), (w_out, l)], name="in_proj")
        y_conv = _short_conv(proj, w_conv[l], width=conv_width)
        o_gla = _gla(proj, a_lr, w_gate_all[l], b_gate[l], gla_norm[l], conv_width=conv_width, dk=dk, dv=dv)
        h, hg, ssq = _matmul([y_conv, o_gla], [(w_out_l, None, 0), (w_out_l, None, conv_width)], out_dtype=_F32,
                             res=h, next_gain=norm_xattn[l], bn=512, bk=conv_width, name="out_proj")
        memn = _rmsnorm(mem2, norm_mem[l], _BF16, 256)
        q = _matmul([hg], [(wq_b, l, 0)], out_dtype=_BF16, row_ssq=ssq, name="xattn_q")
        k = _matmul([memn], [(wk_b, l, 0)], out_dtype=_BF16, name="xattn_k")
        v = _matmul([memn], [(wv_b, l, 0)], out_dtype=_BF16, name="xattn_v")
        att = _xattn(q, k, v)
        h, hg, ssq = _matmul([att], [(wo_b, l, 0)], out_dtype=_F32, res=h, next_gain=norm_mlp[l], name="xattn_o")
        hid, w_down_l = _matmul([hg], [(w_up_l, None, 0)], out_dtype=_BF16, act="relu2", row_ssq=ssq,
                                side_casts=[(w_down, l)], name="mlp_up")
        h = _matmul([hid], [(w_down_l, None, 0)], out_dtype=_F32, res=h, name="mlp_down")
    out = _rmsnorm(h, norm_final, _F32, 512)
    return out.reshape(bsz, seq, d)
```

```python
import functools
import math

import jax
import jax.numpy as jnp
import numpy as np
from jax import lax
from jax.experimental import pallas as pl
from jax.experimental.pallas import tpu as pltpu

EPS = 1e-6
GLA_CHUNK = 64
GLA_GROUP = 4
GLA_HEADS = 4
XATTN_HEADS = 4
GATE_TAU = 16.0
CONV_K = 3
LANES = 128
SUBLANES = 8
MIB = 1024 * 1024
VMEM_CAP = 60 * MIB
VMEM_TEMPS = 12 * MIB
VMEM_DEFAULT = 40 * MIB
LOG2E = math.log2(math.e)

_BF16 = jnp.bfloat16
_F32 = jnp.float32
_NT = (((1,), (1,)), ((), ()))
_TN = (((0,), (0,)), ((), ()))


def _params(sem, block_bytes=None):
    limit = VMEM_DEFAULT if block_bytes is None else min(VMEM_CAP, block_bytes + VMEM_TEMPS)
    return pltpu.CompilerParams(dimension_semantics=sem, vmem_limit_bytes=limit)


def _lane_partial_ssq(h):
    sq = h * h
    acc = sq[:, :LANES]
    for t in range(1, h.shape[1] // LANES):
        acc = acc + sq[:, t * LANES:(t + 1) * LANES]
    return acc


def _row_rsqrt(ssq_ref, inv_d):
    return lax.rsqrt(jnp.sum(ssq_ref[...], axis=-1, keepdims=True) * inv_d + EPS)


def _rmsnorm_body(x_ref, g_ref, o_ref):
    x = x_ref[...]
    y = x * lax.rsqrt(jnp.mean(x * x, axis=-1, keepdims=True) + EPS)
    o_ref[...] = (y * g_ref[...]).astype(o_ref.dtype)


def _rmsnorm(x, gain, out_dtype, bm):
    m, d = x.shape
    bm = min(bm, m)
    return pl.pallas_call(
        _rmsnorm_body,
        out_shape=jax.ShapeDtypeStruct((m, d), out_dtype),
        grid=(m // bm,),
        in_specs=[pl.BlockSpec((bm, d), lambda i: (i, 0)), pl.BlockSpec((1, d), lambda i: (0, 0))],
        out_specs=pl.BlockSpec((bm, d), lambda i: (i, 0)),
        compiler_params=_params(("parallel",)),
        name="rmsnorm",
    )(x, gain.reshape(1, d).astype(_F32))


def _norm_prep_body(x_ref, g_ref, wa_ref, hg_ref, ssq_ref, a_ref, *, inv_d):
    x = x_ref[...]
    hg = (x * g_ref[...]).astype(hg_ref.dtype)
    hg_ref[...] = hg
    ssq = _lane_partial_ssq(x)
    ssq_ref[...] = ssq
    r = lax.rsqrt(jnp.sum(ssq, axis=-1, keepdims=True) * inv_d + EPS)
    a_ref[...] = jnp.dot(hg, wa_ref[...], preferred_element_type=_F32) * r


def _norm_prep(x, gain, w_a, layer, bm=512):
    m, d = x.shape
    na = w_a.shape[2]
    row_block = lambda width: pl.BlockSpec((bm, width), lambda i: (i, 0))
    return pl.pallas_call(
        functools.partial(_norm_prep_body, inv_d=1.0 / d),
        out_shape=(jax.ShapeDtypeStruct((m, d), _BF16), jax.ShapeDtypeStruct((m, LANES), _F32),
                   jax.ShapeDtypeStruct((m, na), _F32)),
        grid=(m // bm,),
        in_specs=[row_block(d), pl.BlockSpec((1, d), lambda i: (0, 0)),
                  pl.BlockSpec((None, d, na), lambda i: (layer, 0, 0))],
        out_specs=(row_block(d), row_block(LANES), row_block(na)),
        compiler_params=_params(("parallel",)),
        name="norm_prep",
    )(x, gain.reshape(1, d).astype(_F32), w_a)


def _mm_body(*refs, npairs, nk, act, has_res, inv_d, emit_norm, nside):
    it = iter(refs)
    xs = [next(it) for _ in range(npairs)]
    ws = [next(it) for _ in range(npairs)]
    res_ref = next(it) if has_res else None
    ssq_in_ref = next(it) if inv_d is not None else None
    gain_ref = next(it) if emit_norm else None
    side_src = [next(it) for _ in range(nside)]
    o_ref = next(it)
    hg_ref, ssq_ref = (next(it), next(it)) if emit_norm else (None, None)
    side_dst = [next(it) for _ in range(nside)]

    for src, dst in zip(side_src, side_dst):
        dst[...] = src[...].astype(dst.dtype)

    acc = jnp.dot(xs[0][...], ws[0][...], preferred_element_type=_F32)
    for p in range(1, npairs):
        acc = acc + jnp.dot(xs[p][...], ws[p][...], preferred_element_type=_F32)

    if nk > 1:
        @pl.when(pl.program_id(2) == 0)
        def _():
            o_ref[...] = res_ref[...] if has_res else jnp.zeros_like(o_ref)

        o_ref[...] += acc
        return

    if inv_d is not None:
        acc = acc * _row_rsqrt(ssq_in_ref, inv_d)
    if act == "relu2":
        r = jnp.maximum(acc, 0.0)
        acc = r * r
    if has_res:
        acc = acc + res_ref[...]
    o_ref[...] = acc.astype(o_ref.dtype)

    if emit_norm:
        hg_ref[...] = (acc * gain_ref[...]).astype(hg_ref.dtype)
        partial = _lane_partial_ssq(acc)
        j = pl.program_id(1)

        @pl.when(j == 0)
        def _():
            ssq_ref[...] = partial

        @pl.when(j > 0)
        def _():
            ssq_ref[...] += partial


def _matmul(xs, ws, *, out_dtype, res=None, act=None, row_ssq=None, next_gain=None, side_casts=(),
            bm=1024, bn=1024, bk=4096, name="matmul"):
    npairs = len(xs)
    m, kdim = xs[0].shape
    n = ws[0][0].shape[-1]
    bm, bn, bk = min(bm, m), min(bn, n), min(bk, kdim)
    nk = kdim // bk
    assert m % bm == 0 and kdim % bk == 0
    assert nk == 1 or (act is None and row_ssq is None and next_gain is None and out_dtype == _F32)
    grid = (m // bm, n // bn, nk)
    nsteps = grid[0] * grid[1] * nk
    in_specs = [pl.BlockSpec((bm, bk), lambda i, j, k: (i, k))] * npairs
    for w, layer, row0 in ws:
        assert row0 % bk == 0
        kb = row0 // bk
        if layer is None:
            in_specs.append(pl.BlockSpec((bk, bn), lambda i, j, k, kb=kb: (kb + k, j)))
        else:
            in_specs.append(pl.BlockSpec((None, bk, bn), lambda i, j, k, kb=kb, layer=layer: (layer, kb + k, j)))
    args = list(xs) + [w for w, _, _ in ws]
    out_bytes = jnp.dtype(out_dtype).itemsize
    block_bytes = 2 * (npairs * 2 * (bm * bk + bk * bn) + bm * bn * out_bytes)
    if res is not None:
        in_specs.append(pl.BlockSpec((bm, bn), lambda i, j, k: (i, j)))
        args.append(res)
        block_bytes += 2 * 4 * bm * bn
    inv_d = None
    if row_ssq is not None:
        in_specs.append(pl.BlockSpec((bm, LANES), lambda i, j, k: (i, 0)))
        args.append(row_ssq)
        inv_d = 1.0 / kdim
        block_bytes += 2 * 4 * bm * LANES
    out_shape = [jax.ShapeDtypeStruct((m, grid[1] * bn), out_dtype)]
    out_specs = [pl.BlockSpec((bm, bn), lambda i, j, k: (i, j))]
    if next_gain is not None:
        assert n % bn == 0
        in_specs.append(pl.BlockSpec((1, bn), lambda i, j, k: (0, j)))
        args.append(next_gain.reshape(1, n).astype(_F32))
        out_shape += [jax.ShapeDtypeStruct((m, n), _BF16), jax.ShapeDtypeStruct((m, LANES), _F32)]
        out_specs += [pl.BlockSpec((bm, bn), lambda i, j, k: (i, j)),
                      pl.BlockSpec((bm, LANES), lambda i, j, k: (i, 0))]
        block_bytes += 2 * (2 * bm * bn + 4 * bm * LANES)
    for src, layer in side_casts:
        _, rows, cols = src.shape
        slab = rows // nsteps
        assert slab * nsteps == rows and slab % (2 * SUBLANES) == 0
        step = lambda i, j, k: (i * grid[1] + j) * nk + k
        in_specs.append(pl.BlockSpec((None, slab, cols), lambda i, j, k, layer=layer: (layer, step(i, j, k), 0)))
        args.append(src)
        out_shape.append(jax.ShapeDtypeStruct((rows, cols), _BF16))
        out_specs.append(pl.BlockSpec((slab, cols), lambda i, j, k: (step(i, j, k), 0)))
        block_bytes += 2 * (4 + 2) * slab * cols
    body = functools.partial(_mm_body, npairs=npairs, nk=nk, act=act, has_res=res is not None, inv_d=inv_d,
                             emit_norm=next_gain is not None, nside=len(side_casts))
    outs = pl.pallas_call(
        body,
        out_shape=tuple(out_shape),
        grid=grid,
        in_specs=in_specs,
        out_specs=tuple(out_specs),
        compiler_params=_params(("parallel", "arbitrary", "arbitrary"), block_bytes),
        name=name,
    )(*args)
    return outs[0] if len(outs) == 1 else outs


def _cast_transposed_body(src_ref, dst_ref):
    dst_ref[...] = src_ref[...].T.astype(dst_ref.dtype)


def _cast_transposed(w_t, bn=256):
    depth, n, kdim = w_t.shape
    return pl.pallas_call(
        _cast_transposed_body,
        out_shape=jax.ShapeDtypeStruct((depth, kdim, n), _BF16),
        grid=(depth, pl.cdiv(n, bn)),
        in_specs=[pl.BlockSpec((None, bn, kdim), lambda l, t: (l, t, 0))],
        out_specs=pl.BlockSpec((None, kdim, bn), lambda l, t: (l, 0, t)),
        compiler_params=_params(("parallel", "parallel")),
        name="cast_transposed",
    )(w_t)


def _conv_body(b_ref, c_ref, h_ref, w_ref, o_ref, tail_ref):
    t = pl.program_id(1)

    @pl.when(t == 0)
    def _():
        tail_ref[...] = jnp.zeros_like(tail_ref)

    u = c_ref[...].astype(_F32) * h_ref[...].astype(_F32)
    rows = u.shape[0]
    row = lax.broadcasted_iota(jnp.int32, u.shape, 0)
    prev1 = tail_ref[SUBLANES - 1:SUBLANES, :]
    prev2 = tail_ref[SUBLANES - 2:SUBLANES - 1, :]
    u1 = jnp.where(row == 0, prev1, pltpu.roll(u, 1, axis=0))
    u2 = jnp.where(row == 0, prev2, jnp.where(row == 1, prev1, pltpu.roll(u, 2, axis=0)))
    w = w_ref[...]
    conv = w[0:1, :] * u2 + w[1:2, :] * u1 + w[2:3, :] * u
    o_ref[...] = (b_ref[...].astype(_F32) * conv).astype(o_ref.dtype)
    tail_ref[...] = u[rows - SUBLANES:, :]


def _short_conv(proj, w_conv, *, width, tb=1024, cb=512):
    s = proj.shape[0]
    ncb = width // cb
    w_pad = jnp.zeros((SUBLANES, width), _F32).at[:CONV_K].set(w_conv.astype(_F32))
    return pl.pallas_call(
        _conv_body,
        out_shape=jax.ShapeDtypeStruct((s, width), _BF16),
        grid=(ncb, s // tb),
        in_specs=[pl.BlockSpec((tb, cb), lambda c, t: (t, c)),
                  pl.BlockSpec((tb, cb), lambda c, t: (t, ncb + c)),
                  pl.BlockSpec((tb, cb), lambda c, t: (t, 2 * ncb + c)),
                  pl.BlockSpec((SUBLANES, cb), lambda c, t: (0, c))],
        out_specs=pl.BlockSpec((tb, cb), lambda c, t: (t, c)),
        scratch_shapes=[pltpu.VMEM((SUBLANES, cb), _F32)],
        compiler_params=_params(("parallel", "arbitrary")),
        name="short_conv",
    )(proj, proj, proj, w_pad)


def _bf16_split_dot(a_exact_bf16, x):
    hi = x.astype(_BF16)
    lo = (x - hi.astype(_F32)).astype(_BF16)
    return (jnp.dot(a_exact_bf16, hi, preferred_element_type=_F32)
            + jnp.dot(a_exact_bf16, lo, preferred_element_type=_F32))


def _gla_constants(L, G):
    nlev = L.bit_length() - 1
    r = np.arange(L * G)
    cum_op = (r[None, :] <= r[:, None]) & (r[None, :] // L == r[:, None] // L)
    i = np.arange(L)
    base = [np.eye(L, dtype=bool)]
    for lev in range(nlev):
        s = L >> lev
        lower = (i % s) >= s // 2
        base.append((i[:, None] // s == i[None, :] // s) & lower[:, None] & ~lower[None, :])
    base = np.stack(base).astype(np.float32)
    masks = np.zeros((nlev + 1, 2, L, 2 * L), np.float32)
    masks[:, 0, :, :L] = base
    masks[:, 1, :, L:] = base
    return jnp.asarray(cum_op, _BF16), jnp.asarray(masks)


def _gla_level_operand(q, k, b, s, row8):
    half = s // 2
    n = b.shape[0]
    parts = []
    if half >= SUBLANES:
        for r0 in range(0, n, half):
            rows = slice(r0, r0 + half)
            r_bound = (r0 // s) * s + half - 1
            b_r = b[r_bound:r_bound + 1, :]
            if (r0 // half) % 2:
                parts.append(q[rows] * jnp.exp2(b[rows] - b_r))
            else:
                parts.append(k[rows] * jnp.exp2(b_r - b[rows]))
    else:
        low = (row8 & (s - 1)) >= half
        sign = jnp.where(low, -1.0, 1.0)
        for r0 in range(0, n, SUBLANES):
            rows = slice(r0, r0 + SUBLANES)
            b8 = b[rows]
            if half == 1:
                b_r = jnp.where(low, pltpu.roll(b8, 1, axis=0), b8)
            else:
                b_r = b8[half - 1:half, :]
                for blk in range(1, SUBLANES // s):
                    b_r = jnp.where(row8 >= blk * s, b8[blk * s + half - 1:blk * s + half, :], b_r)
            parts.append(jnp.where(low, q[rows], k[rows]) * jnp.exp2((b_r - b8) * sign))
    return jnp.concatenate(parts, axis=0).astype(_BF16)


def _gla_prepare_head(q, k, v, la, cum_op, mask_ref, row8):
    L, G = GLA_CHUNK, GLA_GROUP
    nlev = L.bit_length() - 1
    dk = q.shape[1]
    b = _bf16_split_dot(cum_op, la)
    b_last = [b[(c + 1) * L - 1:(c + 1) * L, :] for c in range(G)]
    q_in = (q * jnp.exp2(b)).astype(_BF16)
    k_dec = jnp.concatenate(
        [k[c * L:(c + 1) * L] * jnp.exp2(b_last[c] - b[c * L:(c + 1) * L]) for c in range(G)], axis=0
    ).astype(_BF16)
    decays = [jnp.exp2(bl) for bl in b_last]

    acc = [None] * G

    def add_level(idx, p):
        for c in range(G):
            t = c // 2
            blk = p[c * L:(c + 1) * L, t * LANES:(t + 1) * LANES] * mask_ref[idx, c % 2]
            acc[c] = blk if acc[c] is None else acc[c] + blk

    add_level(0, lax.dot_general(q.astype(_BF16), k.astype(_BF16), _NT, preferred_element_type=_F32))
    for lev in range(nlev):
        x = _gla_level_operand(q, k, b, L >> lev, row8)
        add_level(lev + 1, lax.dot_general(x, x, _NT, preferred_element_type=_F32))

    zeros = jnp.zeros((L, LANES), _F32)
    a_bd = jnp.concatenate(
        [jnp.concatenate([acc[c] if t == c // 2 else zeros for t in range(G // 2)], axis=1) for c in range(G)],
        axis=0).astype(_BF16)
    o_intra = jnp.dot(a_bd, v, preferred_element_type=_F32)
    return o_intra, q_in, k_dec, decays


def _gla_body(q_ref, k_ref, v_ref, g_ref, a_ref, wg_ref, bg_ref, gn_ref, cum_ref, mask_ref, o_ref,
              state_ref, oin_ref, qin_ref, kdec_ref, dec_ref, *, dk, dv):
    L, G = GLA_CHUNK, GLA_GROUP

    @pl.when(pl.program_id(0) == 0)
    def _():
        state_ref[...] = jnp.zeros_like(state_ref)

    z = jnp.dot(a_ref[...].astype(_BF16), wg_ref[...], preferred_element_type=_F32) + bg_ref[...]
    la_all = (jnp.minimum(z, 0.0) - jnp.log(1.0 + jnp.exp(-jnp.abs(z)))) * (LOG2E / GATE_TAU)

    row8 = lax.broadcasted_iota(jnp.int32, (SUBLANES, dk), 0)
    scale = dk ** -0.5
    cum_op = cum_ref[...]
    for hd in range(GLA_HEADS):
        kc = slice(hd * dk, (hd + 1) * dk)
        vc = slice(hd * dv, (hd + 1) * dv)
        o_intra, q_in, k_dec, decays = _gla_prepare_head(
            q_ref[:, kc].astype(_F32) * scale, k_ref[:, kc].astype(_F32), v_ref[:, vc], la_all[:, kc],
            cum_op, mask_ref, row8)
        oin_ref[:, vc] = o_intra
        qin_ref[:, kc] = q_in
        kdec_ref[:, kc] = k_dec
        for c in range(G):
            dec_ref[c:c + 1, kc] = decays[c]

    for c in range(G):
        rows = slice(c * L, (c + 1) * L)
        for hd in range(GLA_HEADS):
            kc = slice(hd * dk, (hd + 1) * dk)
            vc = slice(hd * dv, (hd + 1) * dv)
            st = state_ref[hd]
            o = oin_ref[rows, vc] + lax.dot_general(qin_ref[rows, kc], st.astype(_BF16), _NT,
                                                    preferred_element_type=_F32)
            upd = lax.dot_general(v_ref[rows, vc], kdec_ref[rows, kc], _TN, preferred_element_type=_F32)
            state_ref[hd] = st * dec_ref[c:c + 1, kc] + upd
            o = o * lax.rsqrt(jnp.mean(o * o, axis=-1, keepdims=True) + EPS)
            g = g_ref[rows, vc].astype(_F32)
            o_ref[rows, vc] = (o * gn_ref[:, vc] * (g * jax.nn.sigmoid(g))).astype(o_ref.dtype)


def _gla(proj, a_lr, w_gate_pad, b_gate, gla_norm, *, conv_width, dk, dv):
    s = proj.shape[0]
    h = GLA_HEADS
    tb = GLA_CHUNK * GLA_GROUP
    kw, vw = h * dk, h * dv
    q_blk = (3 * conv_width) // kw
    v_blk = (3 * conv_width + 2 * kw) // vw
    assert q_blk * kw == 3 * conv_width and v_blk * vw == 3 * conv_width + 2 * kw
    assert 2 * GLA_CHUNK == LANES and GLA_GROUP % 2 == 0 and s % tb == 0
    na = a_lr.shape[1]
    cum_op, pair_mask = _gla_constants(GLA_CHUNK, GLA_GROUP)
    body = functools.partial(_gla_body, dk=dk, dv=dv)
    return pl.pallas_call(
        body,
        out_shape=jax.ShapeDtypeStruct((s, vw), _BF16),
        grid=(s // tb,),
        in_specs=[pl.BlockSpec((tb, kw), lambda t: (t, q_blk)),
                  pl.BlockSpec((tb, kw), lambda t: (t, q_blk + 1)),
                  pl.BlockSpec((tb, vw), lambda t: (t, v_blk)),
                  pl.BlockSpec((tb, vw), lambda t: (t, v_blk + 1)),
                  pl.BlockSpec((tb, na), lambda t: (t, 0)),
                  pl.BlockSpec((na, kw), lambda t: (0, 0)),
                  pl.BlockSpec((1, kw), lambda t: (0, 0)),
                  pl.BlockSpec((1, vw), lambda t: (0, 0)),
                  pl.BlockSpec(cum_op.shape, lambda t: (0, 0)),
                  pl.BlockSpec(pair_mask.shape, lambda t: (0, 0, 0, 0))],
        out_specs=pl.BlockSpec((tb, vw), lambda t: (t, 0)),
        scratch_shapes=[pltpu.VMEM((h, dv, dk), _F32),
                        pltpu.VMEM((tb, vw), _F32),
                        pltpu.VMEM((tb, kw), _BF16),
                        pltpu.VMEM((tb, kw), _BF16),
                        pltpu.VMEM((SUBLANES, kw), _F32)],
        compiler_params=_params(("arbitrary",)),
        name="gla",
    )(proj, proj, proj, proj, a_lr, w_gate_pad, b_gate.reshape(1, -1).astype(_F32),
      gla_norm.reshape(1, -1).astype(_F32), cum_op, pair_mask)


def _xattn_body(hg_ref, ssq_ref, res_ref, wq_ref, k_ref, v_ref, wo_ref, gain_ref,
                o_ref, hgo_ref, ssqo_ref, *, dh, inv_d):
    scale = dh ** -0.5
    q = jnp.dot(hg_ref[...], wq_ref[...], preferred_element_type=_F32) * _row_rsqrt(ssq_ref, inv_d)
    q = q.astype(_BF16)
    heads = []
    for hd in range(XATTN_HEADS):
        sl = slice(hd * dh, (hd + 1) * dh)
        sc = lax.dot_general(q[:, sl], k_ref[:, sl], _NT, preferred_element_type=_F32) * scale
        p = jnp.exp(sc - jnp.max(sc, axis=-1, keepdims=True))
        denom = jnp.sum(p, axis=-1, keepdims=True)
        pv = jnp.dot(p.astype(_BF16), v_ref[:, sl], preferred_element_type=_F32)
        heads.append((pv / denom).astype(_BF16))
    att = jnp.concatenate(heads, axis=1)
    h = jnp.dot(att, wo_ref[...], preferred_element_type=_F32) + res_ref[...]
    o_ref[...] = h
    hgo_ref[...] = (h * gain_ref[...]).astype(hgo_ref.dtype)
    ssqo_ref[...] = _lane_partial_ssq(h)


def _xattn(hg, ssq, res, wq, wo, layer, k, v, next_gain, *, bm=256):
    s, d = hg.shape
    mlen, width = k.shape
    body = functools.partial(_xattn_body, dh=width // XATTN_HEADS, inv_d=1.0 / d)
    row_block = lambda w: pl.BlockSpec((bm, w), lambda i: (i, 0))
    whole = lambda a: pl.BlockSpec(a.shape, lambda i: (0, 0))
    w_block = lambda a: pl.BlockSpec((None,) + a.shape[1:], lambda i: (layer, 0, 0), pipeline_mode=pl.Buffered(1))
    block_bytes = (2 * bm * (2 * d + 4 * LANES + 4 * d) + 2 * bm * (4 * d + 2 * d + 4 * LANES)
                   + 2 * 2 * d * width + 2 * 2 * 2 * mlen * width)
    return pl.pallas_call(
        body,
        out_shape=(jax.ShapeDtypeStruct((s, d), _F32), jax.ShapeDtypeStruct((s, d), _BF16),
                   jax.ShapeDtypeStruct((s, LANES), _F32)),
        grid=(s // bm,),
        in_specs=[row_block(d), row_block(LANES), row_block(d), w_block(wq), whole(k), whole(v), w_block(wo),
                  pl.BlockSpec((1, d), lambda i: (0, 0))],
        out_specs=(row_block(d), row_block(d), row_block(LANES)),
        compiler_params=_params(("parallel",), block_bytes),
        name="xattn",
    )(hg, ssq, res, wq, k, v, wo, next_gain.reshape(1, d).astype(_F32))


def kernel(x, mem, norm_mix, w_in, w_conv, w_gate_up, b_gate, gla_norm, w_out, norm_xattn, norm_mem,
           wq_x, wk_x, wv_x, wo_x, norm_mlp, w_up, w_down, norm_final):
    bsz, seq, d = x.shape
    depth = w_in.shape[0]
    conv_width = w_conv.shape[2]
    gla_key = w_gate_up.shape[2]
    gla_width = gla_norm.shape[1]
    gate_rank = w_gate_up.shape[1]
    dk = gla_key // GLA_HEADS
    dv = gla_width // GLA_HEADS
    main_cols = 3 * conv_width + 2 * gla_key + 2 * gla_width
    assert bsz == 1 and w_in.shape[2] == main_cols + gate_rank

    wq_b, wk_b, wv_b, wo_b = (w.astype(_BF16) for w in (wq_x, wk_x, wv_x, wo_x))
    w_in_b = _cast_transposed(jnp.swapaxes(w_in, 1, 2))
    w_a_all = jnp.zeros((depth, d, LANES), _BF16).at[:, :, :gate_rank].set(w_in_b[:, :, main_cols:])
    w_gate_all = jnp.zeros((depth, LANES, gla_key), _BF16).at[:, :gate_rank].set(w_gate_up.astype(_BF16))

    h = x.reshape(seq, d)
    mem2 = mem.reshape(mem.shape[1], d)
    for l in range(depth):
        hg, ssq, a_lr = _norm_prep(h, norm_mix[l], w_a_all, l)
        proj, w_up_l, w_out_l = _matmul([hg], [(w_in_b, l, 0)], out_dtype=_BF16, row_ssq=ssq, bn=768,
                                        side_casts=[(w_up, l), (w_out, l)], name="in_proj")
        y_conv = _short_conv(proj, w_conv[l], width=conv_width)
        o_gla = _gla(proj, a_lr, w_gate_all[l], b_gate[l], gla_norm[l], conv_width=conv_width, dk=dk, dv=dv)
        h, hg, ssq = _matmul([y_conv, o_gla], [(w_out_l, None, 0), (w_out_l, None, conv_width)], out_dtype=_F32,
                             res=h, next_gain=norm_xattn[l], bn=512, bk=conv_width, name="out_proj")
        memn = _rmsnorm(mem2, norm_mem[l], _BF16, 256)
        k = _matmul([memn], [(wk_b, l, 0)], out_dtype=_BF16, name="xattn_k")
        v = _matmul([memn], [(wv_b, l, 0)], out_dtype=_BF16, name="xattn_v")
        h, hg, ssq = _xattn(hg, ssq, h, wq_b, wo_b, l, k, v, norm_mlp[l])
        hid, w_down_l = _matmul([hg], [(w_up_l, None, 0)], out_dtype=_BF16, act="relu2", row_ssq=ssq,
                                side_casts=[(w_down, l)], name="mlp_up")
        h = _matmul([hid], [(w_down_l, None, 0)], out_dtype=_F32, res=h, name="mlp_down")
    out = _rmsnorm(h, norm_final, _F32, 512)
    return out.reshape(bsz, seq, d)
```

```python
import functools
import math

import jax
import jax.numpy as jnp
import numpy as np
from jax import lax
from jax.experimental import pallas as pl
from jax.experimental.pallas import tpu as pltpu

EPS = 1e-6
GLA_CHUNK = 64
GLA_GROUP = 4
GLA_HEADS = 4
XATTN_HEADS = 4
GATE_TAU = 16.0
CONV_K = 3
LANES = 128
SUBLANES = 8
MIB = 1024 * 1024
VMEM_CAP = 60 * MIB
VMEM_TEMPS = 12 * MIB
VMEM_DEFAULT = 40 * MIB
LOG2E = math.log2(math.e)

_BF16 = jnp.bfloat16
_F32 = jnp.float32
_NT = (((1,), (1,)), ((), ()))
_TN = (((0,), (0,)), ((), ()))


def _params(sem, block_bytes=None):
    limit = VMEM_DEFAULT if block_bytes is None else min(VMEM_CAP, block_bytes + VMEM_TEMPS)
    return pltpu.CompilerParams(dimension_semantics=sem, vmem_limit_bytes=limit)


def _lane_partial_ssq(h):
    sq = h * h
    acc = sq[:, :LANES]
    for t in range(1, h.shape[1] // LANES):
        acc = acc + sq[:, t * LANES:(t + 1) * LANES]
    return acc


def _row_rsqrt(ssq_ref, inv_d):
    return lax.rsqrt(jnp.sum(ssq_ref[...], axis=-1, keepdims=True) * inv_d + EPS)


def _rmsnorm_body(x_ref, g_ref, o_ref):
    x = x_ref[...]
    y = x * lax.rsqrt(jnp.mean(x * x, axis=-1, keepdims=True) + EPS)
    o_ref[...] = (y * g_ref[...]).astype(o_ref.dtype)


def _rmsnorm(x, gain, out_dtype, bm):
    m, d = x.shape
    bm = min(bm, m)
    return pl.pallas_call(
        _rmsnorm_body,
        out_shape=jax.ShapeDtypeStruct((m, d), out_dtype),
        grid=(m // bm,),
        in_specs=[pl.BlockSpec((bm, d), lambda i: (i, 0)), pl.BlockSpec((1, d), lambda i: (0, 0))],
        out_specs=pl.BlockSpec((bm, d), lambda i: (i, 0)),
        compiler_params=_params(("parallel",)),
        name="rmsnorm",
    )(x, gain.reshape(1, d).astype(_F32))


def _norm_prep_body(x_ref, g_ref, wa_ref, hg_ref, ssq_ref, a_ref, *, inv_d):
    x = x_ref[...]
    hg = (x * g_ref[...]).astype(hg_ref.dtype)
    hg_ref[...] = hg
    ssq = _lane_partial_ssq(x)
    ssq_ref[...] = ssq
    r = lax.rsqrt(jnp.sum(ssq, axis=-1, keepdims=True) * inv_d + EPS)
    a_ref[...] = jnp.dot(hg, wa_ref[...], preferred_element_type=_F32) * r


def _norm_prep(x, gain, w_a, bm=512):
    m, d = x.shape
    na = w_a.shape[1]
    row_block = lambda width: pl.BlockSpec((bm, width), lambda i: (i, 0))
    return pl.pallas_call(
        functools.partial(_norm_prep_body, inv_d=1.0 / d),
        out_shape=(jax.ShapeDtypeStruct((m, d), _BF16), jax.ShapeDtypeStruct((m, LANES), _F32),
                   jax.ShapeDtypeStruct((m, na), _F32)),
        grid=(m // bm,),
        in_specs=[row_block(d), pl.BlockSpec((1, d), lambda i: (0, 0)),
                  pl.BlockSpec((d, na), lambda i: (0, 0))],
        out_specs=(row_block(d), row_block(LANES), row_block(na)),
        compiler_params=_params(("parallel",)),
        name="norm_prep",
    )(x, gain.reshape(1, d).astype(_F32), w_a)


def _mm_body(*refs, npairs, grid, act, has_res, inv_d, emit_norm, cast_w, sides):
    it = iter(refs)
    xs = [next(it) for _ in range(npairs)]
    ws = [next(it) for _ in range(npairs)]
    res_ref = next(it) if has_res else None
    ssq_in_ref = next(it) if inv_d is not None else None
    gain_ref = next(it) if emit_norm else None
    side_src = [next(it) for _ in sides]
    o_ref = next(it)
    hg_ref, ssq_ref = (next(it), next(it)) if emit_norm else (None, None)
    side_dst = [next(it) for _ in sides]
    nk = grid[2]

    step = (pl.program_id(0) * grid[1] + pl.program_id(1)) * nk + pl.program_id(2)
    for (transposed, nblocks), src, dst in zip(sides, side_src, side_dst):
        def cast(src=src, dst=dst, transposed=transposed):
            val = src[...]
            dst[...] = (val.T if transposed else val).astype(dst.dtype)

        if nblocks == grid[0] * grid[1] * nk:
            cast()
        else:
            pl.when(step < nblocks)(cast)

    def weight(w_ref):
        return w_ref[...].astype(_BF16) if cast_w else w_ref[...]

    acc = jnp.dot(xs[0][...], weight(ws[0]), preferred_element_type=_F32)
    for p in range(1, npairs):
        acc = acc + jnp.dot(xs[p][...], weight(ws[p]), preferred_element_type=_F32)

    if nk > 1:
        @pl.when(pl.program_id(2) == 0)
        def _():
            o_ref[...] = res_ref[...] if has_res else jnp.zeros_like(o_ref)

        o_ref[...] += acc
        return

    if inv_d is not None:
        acc = acc * _row_rsqrt(ssq_in_ref, inv_d)
    if act == "relu2":
        r = jnp.maximum(acc, 0.0)
        acc = r * r
    if has_res:
        acc = acc + res_ref[...]
    o_ref[...] = acc.astype(o_ref.dtype)

    if emit_norm:
        hg_ref[...] = (acc * gain_ref[...]).astype(hg_ref.dtype)
        partial = _lane_partial_ssq(acc)
        j = pl.program_id(1)

        @pl.when(j == 0)
        def _():
            ssq_ref[...] = partial

        @pl.when(j > 0)
        def _():
            ssq_ref[...] += partial


def _matmul(xs, ws, *, out_dtype, res=None, act=None, row_ssq=None, next_gain=None, side_casts=(),
            bm=1024, bn=1024, bk=4096, name="matmul"):
    cast_w = ws[0][0].dtype == _F32
    w_bytes = 4 if cast_w else 2
    npairs = len(xs)
    m, kdim = xs[0].shape
    n = ws[0][0].shape[-1]
    bm, bn, bk = min(bm, m), min(bn, n), min(bk, kdim)
    nk = kdim // bk
    assert m % bm == 0 and kdim % bk == 0
    assert nk == 1 or (act is None and row_ssq is None and next_gain is None and out_dtype == _F32)
    grid = (m // bm, n // bn, nk)
    nsteps = grid[0] * grid[1] * nk
    in_specs = [pl.BlockSpec((bm, bk), lambda i, j, k: (i, k))] * npairs
    for w, layer, row0 in ws:
        assert row0 % bk == 0
        kb = row0 // bk
        if layer is None:
            in_specs.append(pl.BlockSpec((bk, bn), lambda i, j, k, kb=kb: (kb + k, j)))
        else:
            in_specs.append(pl.BlockSpec((None, bk, bn), lambda i, j, k, kb=kb, layer=layer: (layer, kb + k, j)))
    args = list(xs) + [w for w, _, _ in ws]
    out_bytes = jnp.dtype(out_dtype).itemsize
    block_bytes = 2 * (npairs * (2 * bm * bk + w_bytes * bk * bn) + bm * bn * out_bytes)
    if res is not None:
        in_specs.append(pl.BlockSpec((bm, bn), lambda i, j, k: (i, j)))
        args.append(res)
        block_bytes += 2 * 4 * bm * bn
    inv_d = None
    if row_ssq is not None:
        in_specs.append(pl.BlockSpec((bm, LANES), lambda i, j, k: (i, 0)))
        args.append(row_ssq)
        inv_d = 1.0 / kdim
        block_bytes += 2 * 4 * bm * LANES
    out_shape = [jax.ShapeDtypeStruct((m, grid[1] * bn), out_dtype)]
    out_specs = [pl.BlockSpec((bm, bn), lambda i, j, k: (i, j))]
    if next_gain is not None:
        assert n % bn == 0
        in_specs.append(pl.BlockSpec((1, bn), lambda i, j, k: (0, j)))
        args.append(next_gain.reshape(1, n).astype(_F32))
        out_shape += [jax.ShapeDtypeStruct((m, n), _BF16), jax.ShapeDtypeStruct((m, LANES), _F32)]
        out_specs += [pl.BlockSpec((bm, bn), lambda i, j, k: (i, j)),
                      pl.BlockSpec((bm, LANES), lambda i, j, k: (i, 0))]
        block_bytes += 2 * (2 * bm * bn + 4 * bm * LANES)
    sides = []
    for src, layer, transposed in side_casts:
        _, rows, cols = src.shape
        if transposed:
            slab = LANES * pl.cdiv(pl.cdiv(rows, LANES), nsteps)
            nblocks = pl.cdiv(rows, slab)
            out_shape.append(jax.ShapeDtypeStruct((cols, rows), _BF16))
        else:
            slab = rows // nsteps if rows % (nsteps * 2 * SUBLANES) == 0 else 2 * SUBLANES
            assert rows % slab == 0
            nblocks = rows // slab
            out_shape.append(jax.ShapeDtypeStruct((rows, cols), _BF16))
        assert nblocks <= nsteps
        blk = lambda i, j, k, last=nblocks - 1: jnp.minimum((i * grid[1] + j) * nk + k, last)
        in_specs.append(pl.BlockSpec((None, slab, cols), lambda i, j, k, layer=layer, blk=blk: (layer, blk(i, j, k), 0)))
        args.append(src)
        if transposed:
            out_specs.append(pl.BlockSpec((cols, slab), lambda i, j, k, blk=blk: (0, blk(i, j, k))))
        else:
            out_specs.append(pl.BlockSpec((slab, cols), lambda i, j, k, blk=blk: (blk(i, j, k), 0)))
        sides.append((transposed, nblocks))
        block_bytes += 2 * (4 + 2) * slab * cols
    body = functools.partial(_mm_body, npairs=npairs, grid=grid, act=act, has_res=res is not None, inv_d=inv_d,
                             emit_norm=next_gain is not None, cast_w=cast_w, sides=tuple(sides))
    outs = pl.pallas_call(
        body,
        out_shape=tuple(out_shape),
        grid=grid,
        in_specs=in_specs,
        out_specs=tuple(out_specs),
        compiler_params=_params(("parallel", "arbitrary", "arbitrary"), block_bytes),
        name=name,
    )(*args)
    return outs[0] if len(outs) == 1 else outs


def _cast_transposed_body(src_ref, dst_ref):
    dst_ref[...] = src_ref[...].T.astype(dst_ref.dtype)


def _cast_transposed(w_t, layer, bn=256):
    _, n, kdim = w_t.shape
    return pl.pallas_call(
        _cast_transposed_body,
        out_shape=jax.ShapeDtypeStruct((kdim, n), _BF16),
        grid=(pl.cdiv(n, bn),),
        in_specs=[pl.BlockSpec((None, bn, kdim), lambda t: (layer, t, 0))],
        out_specs=pl.BlockSpec((kdim, bn), lambda t: (0, t)),
        compiler_params=_params(("parallel",)),
        name="cast_transposed",
    )(w_t)


def _conv_body(b_ref, c_ref, h_ref, w_ref, o_ref, tail_ref):
    t = pl.program_id(1)

    @pl.when(t == 0)
    def _():
        tail_ref[...] = jnp.zeros_like(tail_ref)

    u = c_ref[...].astype(_F32) * h_ref[...].astype(_F32)
    rows = u.shape[0]
    row = lax.broadcasted_iota(jnp.int32, u.shape, 0)
    prev1 = tail_ref[SUBLANES - 1:SUBLANES, :]
    prev2 = tail_ref[SUBLANES - 2:SUBLANES - 1, :]
    u1 = jnp.where(row == 0, prev1, pltpu.roll(u, 1, axis=0))
    u2 = jnp.where(row == 0, prev2, jnp.where(row == 1, prev1, pltpu.roll(u, 2, axis=0)))
    w = w_ref[...]
    conv = w[0:1, :] * u2 + w[1:2, :] * u1 + w[2:3, :] * u
    o_ref[...] = (b_ref[...].astype(_F32) * conv).astype(o_ref.dtype)
    tail_ref[...] = u[rows - SUBLANES:, :]


def _short_conv(proj, w_conv, *, width, tb=1024, cb=512):
    s = proj.shape[0]
    ncb = width // cb
    w_pad = jnp.zeros((SUBLANES, width), _F32).at[:CONV_K].set(w_conv.astype(_F32))
    return pl.pallas_call(
        _conv_body,
        out_shape=jax.ShapeDtypeStruct((s, width), _BF16),
        grid=(ncb, s // tb),
        in_specs=[pl.BlockSpec((tb, cb), lambda c, t: (t, c)),
                  pl.BlockSpec((tb, cb), lambda c, t: (t, ncb + c)),
                  pl.BlockSpec((tb, cb), lambda c, t: (t, 2 * ncb + c)),
                  pl.BlockSpec((SUBLANES, cb), lambda c, t: (0, c))],
        out_specs=pl.BlockSpec((tb, cb), lambda c, t: (t, c)),
        scratch_shapes=[pltpu.VMEM((SUBLANES, cb), _F32)],
        compiler_params=_params(("parallel", "arbitrary")),
        name="short_conv",
    )(proj, proj, proj, w_pad)


def _bf16_split_dot(a_exact_bf16, x):
    hi = x.astype(_BF16)
    lo = (x - hi.astype(_F32)).astype(_BF16)
    return (jnp.dot(a_exact_bf16, hi, preferred_element_type=_F32)
            + jnp.dot(a_exact_bf16, lo, preferred_element_type=_F32))


def _gla_constants(L, G):
    nlev = L.bit_length() - 1
    r = np.arange(L * G)
    cum_op = (r[None, :] <= r[:, None]) & (r[None, :] // L == r[:, None] // L)
    i = np.arange(L)
    base = [np.eye(L, dtype=bool)]
    for lev in range(nlev):
        s = L >> lev
        lower = (i % s) >= s // 2
        base.append((i[:, None] // s == i[None, :] // s) & lower[:, None] & ~lower[None, :])
    base = np.stack(base).astype(np.float32)
    masks = np.zeros((nlev + 1, 2, L, 2 * L), np.float32)
    masks[:, 0, :, :L] = base
    masks[:, 1, :, L:] = base
    return jnp.asarray(cum_op, _BF16), jnp.asarray(masks)


def _gla_level_operand(q, k, b, s, row8):
    half = s // 2
    n = b.shape[0]
    parts = []
    if half >= SUBLANES:
        for r0 in range(0, n, half):
            rows = slice(r0, r0 + half)
            r_bound = (r0 // s) * s + half - 1
            b_r = b[r_bound:r_bound + 1, :]
            if (r0 // half) % 2:
                parts.append(q[rows] * jnp.exp2(b[rows] - b_r))
            else:
                parts.append(k[rows] * jnp.exp2(b_r - b[rows]))
    else:
        low = (row8 & (s - 1)) >= half
        sign = jnp.where(low, -1.0, 1.0)
        for r0 in range(0, n, SUBLANES):
            rows = slice(r0, r0 + SUBLANES)
            b8 = b[rows]
            if half == 1:
                b_r = jnp.where(low, pltpu.roll(b8, 1, axis=0), b8)
            else:
                b_r = b8[half - 1:half, :]
                for blk in range(1, SUBLANES // s):
                    b_r = jnp.where(row8 >= blk * s, b8[blk * s + half - 1:blk * s + half, :], b_r)
            parts.append(jnp.where(low, q[rows], k[rows]) * jnp.exp2((b_r - b8) * sign))
    return jnp.concatenate(parts, axis=0).astype(_BF16)


def _gla_prepare_head(q, k, v, la, cum_op, mask_ref, row8):
    L, G = GLA_CHUNK, GLA_GROUP
    nlev = L.bit_length() - 1
    dk = q.shape[1]
    b = _bf16_split_dot(cum_op, la)
    b_last = [b[(c + 1) * L - 1:(c + 1) * L, :] for c in range(G)]
    q_in = (q * jnp.exp2(b)).astype(_BF16)
    k_dec = jnp.concatenate(
        [k[c * L:(c + 1) * L] * jnp.exp2(b_last[c] - b[c * L:(c + 1) * L]) for c in range(G)], axis=0
    ).astype(_BF16)
    decays = [jnp.exp2(bl) for bl in b_last]

    acc = [None] * G

    def add_level(idx, p):
        for c in range(G):
            t = c // 2
            blk = p[c * L:(c + 1) * L, t * LANES:(t + 1) * LANES] * mask_ref[idx, c % 2]
            acc[c] = blk if acc[c] is None else acc[c] + blk

    add_level(0, lax.dot_general(q.astype(_BF16), k.astype(_BF16), _NT, preferred_element_type=_F32))
    for lev in range(nlev):
        x = _gla_level_operand(q, k, b, L >> lev, row8)
        add_level(lev + 1, lax.dot_general(x, x, _NT, preferred_element_type=_F32))

    zeros = jnp.zeros((L, LANES), _F32)
    a_bd = jnp.concatenate(
        [jnp.concatenate([acc[c] if t == c // 2 else zeros for t in range(G // 2)], axis=1) for c in range(G)],
        axis=0).astype(_BF16)
    o_intra = jnp.dot(a_bd, v, preferred_element_type=_F32)
    return o_intra, q_in, k_dec, decays


def _gla_body(q_ref, k_ref, v_ref, g_ref, a_ref, wg_ref, bg_ref, gn_ref, cum_ref, mask_ref, o_ref,
              state_ref, oin_ref, qin_ref, kdec_ref, dec_ref, *, dk, dv):
    L, G = GLA_CHUNK, GLA_GROUP

    @pl.when(pl.program_id(0) == 0)
    def _():
        state_ref[...] = jnp.zeros_like(state_ref)

    z = jnp.dot(a_ref[...].astype(_BF16), wg_ref[...], preferred_element_type=_F32) + bg_ref[...]
    la_all = (jnp.minimum(z, 0.0) - jnp.log(1.0 + jnp.exp(-jnp.abs(z)))) * (LOG2E / GATE_TAU)

    row8 = lax.broadcasted_iota(jnp.int32, (SUBLANES, dk), 0)
    scale = dk ** -0.5
    cum_op = cum_ref[...]
    for hd in range(GLA_HEADS):
        kc = slice(hd * dk, (hd + 1) * dk)
        vc = slice(hd * dv, (hd + 1) * dv)
        o_intra, q_in, k_dec, decays = _gla_prepare_head(
            q_ref[:, kc].astype(_F32) * scale, k_ref[:, kc].astype(_F32), v_ref[:, vc], la_all[:, kc],
            cum_op, mask_ref, row8)
        oin_ref[:, vc] = o_intra
        qin_ref[:, kc] = q_in
        kdec_ref[:, kc] = k_dec
        for c in range(G):
            dec_ref[c:c + 1, kc] = decays[c]

    for c in range(G):
        rows = slice(c * L, (c + 1) * L)
        for hd in range(GLA_HEADS):
            kc = slice(hd * dk, (hd + 1) * dk)
            vc = slice(hd * dv, (hd + 1) * dv)
            st = state_ref[hd]
            o = oin_ref[rows, vc] + lax.dot_general(qin_ref[rows, kc], st.astype(_BF16), _NT,
                                                    preferred_element_type=_F32)
            upd = lax.dot_general(v_ref[rows, vc], kdec_ref[rows, kc], _TN, preferred_element_type=_F32)
            state_ref[hd] = st * dec_ref[c:c + 1, kc] + upd
            o = o * lax.rsqrt(jnp.mean(o * o, axis=-1, keepdims=True) + EPS)
            g = g_ref[rows, vc].astype(_F32)
            o_ref[rows, vc] = (o * gn_ref[:, vc] * (g * jax.nn.sigmoid(g))).astype(o_ref.dtype)


def _gla(proj, a_lr, w_gate_pad, b_gate, gla_norm, *, conv_width, dk, dv):
    s = proj.shape[0]
    h = GLA_HEADS
    tb = GLA_CHUNK * GLA_GROUP
    kw, vw = h * dk, h * dv
    q_blk = (3 * conv_width) // kw
    v_blk = (3 * conv_width + 2 * kw) // vw
    assert q_blk * kw == 3 * conv_width and v_blk * vw == 3 * conv_width + 2 * kw
    assert 2 * GLA_CHUNK == LANES and GLA_GROUP % 2 == 0 and s % tb == 0
    na = a_lr.shape[1]
    cum_op, pair_mask = _gla_constants(GLA_CHUNK, GLA_GROUP)
    body = functools.partial(_gla_body, dk=dk, dv=dv)
    return pl.pallas_call(
        body,
        out_shape=jax.ShapeDtypeStruct((s, vw), _BF16),
        grid=(s // tb,),
        in_specs=[pl.BlockSpec((tb, kw), lambda t: (t, q_blk)),
                  pl.BlockSpec((tb, kw), lambda t: (t, q_blk + 1)),
                  pl.BlockSpec((tb, vw), lambda t: (t, v_blk)),
                  pl.BlockSpec((tb, vw), lambda t: (t, v_blk + 1)),
                  pl.BlockSpec((tb, na), lambda t: (t, 0)),
                  pl.BlockSpec((na, kw), lambda t: (0, 0)),
                  pl.BlockSpec((1, kw), lambda t: (0, 0)),
                  pl.BlockSpec((1, vw), lambda t: (0, 0)),
                  pl.BlockSpec(cum_op.shape, lambda t: (0, 0)),
                  pl.BlockSpec(pair_mask.shape, lambda t: (0, 0, 0, 0))],
        out_specs=pl.BlockSpec((tb, vw), lambda t: (t, 0)),
        scratch_shapes=[pltpu.VMEM((h, dv, dk), _F32),
                        pltpu.VMEM((tb, vw), _F32),
                        pltpu.VMEM((tb, kw), _BF16),
                        pltpu.VMEM((tb, kw), _BF16),
                        pltpu.VMEM((SUBLANES, kw), _F32)],
        compiler_params=_params(("arbitrary",)),
        name="gla",
    )(proj, proj, proj, proj, a_lr, w_gate_pad, b_gate.reshape(1, -1).astype(_F32),
      gla_norm.reshape(1, -1).astype(_F32), cum_op, pair_mask)


def _xattn_body(hg_ref, ssq_ref, res_ref, wq_ref, k_ref, v_ref, wo_ref, gain_ref,
                o_ref, hgo_ref, ssqo_ref, *, dh, inv_d):
    scale = dh ** -0.5
    q = jnp.dot(hg_ref[...], wq_ref[...], preferred_element_type=_F32) * _row_rsqrt(ssq_ref, inv_d)
    q = q.astype(_BF16)
    heads = []
    for hd in range(XATTN_HEADS):
        sl = slice(hd * dh, (hd + 1) * dh)
        sc = lax.dot_general(q[:, sl], k_ref[:, sl], _NT, preferred_element_type=_F32) * scale
        p = jnp.exp(sc - jnp.max(sc, axis=-1, keepdims=True))
        denom = jnp.sum(p, axis=-1, keepdims=True)
        pv = jnp.dot(p.astype(_BF16), v_ref[:, sl], preferred_element_type=_F32)
        heads.append((pv / denom).astype(_BF16))
    att = jnp.concatenate(heads, axis=1)
    h = jnp.dot(att, wo_ref[...], preferred_element_type=_F32) + res_ref[...]
    o_ref[...] = h
    hgo_ref[...] = (h * gain_ref[...]).astype(hgo_ref.dtype)
    ssqo_ref[...] = _lane_partial_ssq(h)


def _xattn(hg, ssq, res, wq, wo, k, v, next_gain, *, bm=256):
    s, d = hg.shape
    mlen, width = k.shape
    body = functools.partial(_xattn_body, dh=width // XATTN_HEADS, inv_d=1.0 / d)
    row_block = lambda w: pl.BlockSpec((bm, w), lambda i: (i, 0))
    whole = lambda a: pl.BlockSpec(a.shape, lambda i: (0, 0))
    w_block = lambda a: pl.BlockSpec(a.shape, lambda i: (0, 0), pipeline_mode=pl.Buffered(1))
    block_bytes = (2 * bm * (2 * d + 4 * LANES + 4 * d) + 2 * bm * (4 * d + 2 * d + 4 * LANES)
                   + 2 * 2 * d * width + 2 * 2 * 2 * mlen * width)
    return pl.pallas_call(
        body,
        out_shape=(jax.ShapeDtypeStruct((s, d), _F32), jax.ShapeDtypeStruct((s, d), _BF16),
                   jax.ShapeDtypeStruct((s, LANES), _F32)),
        grid=(s // bm,),
        in_specs=[row_block(d), row_block(LANES), row_block(d), w_block(wq), whole(k), whole(v), w_block(wo),
                  pl.BlockSpec((1, d), lambda i: (0, 0))],
        out_specs=(row_block(d), row_block(d), row_block(LANES)),
        compiler_params=_params(("parallel",), block_bytes),
        name="xattn",
    )(hg, ssq, res, wq, k, v, wo, next_gain.reshape(1, d).astype(_F32))


def kernel(x, mem, norm_mix, w_in, w_conv, w_gate_up, b_gate, gla_norm, w_out, norm_xattn, norm_mem,
           wq_x, wk_x, wv_x, wo_x, norm_mlp, w_up, w_down, norm_final):
    bsz, seq, d = x.shape
    depth = w_in.shape[0]
    conv_width = w_conv.shape[2]
    gla_key = w_gate_up.shape[2]
    gla_width = gla_norm.shape[1]
    gate_rank = w_gate_up.shape[1]
    dk = gla_key // GLA_HEADS
    dv = gla_width // GLA_HEADS
    main_cols = 3 * conv_width + 2 * gla_key + 2 * gla_width
    assert bsz == 1 and w_in.shape[2] == main_cols + gate_rank

    w_gate_all = jnp.zeros((depth, LANES, gla_key), _BF16).at[:, :gate_rank].set(w_gate_up.astype(_BF16))
    w_in_t = jnp.swapaxes(w_in, 1, 2)
    w_in_l = _cast_transposed(w_in_t, 0)

    h = x.reshape(seq, d)
    mem2 = mem.reshape(mem.shape[1], d)
    for l in range(depth):
        w_a = jnp.zeros((d, LANES), _BF16).at[:, :gate_rank].set(w_in_l[:, main_cols:])
        hg, ssq, a_lr = _norm_prep(h, norm_mix[l], w_a)
        proj, w_up_l, w_out_l, wq_l, wo_l = _matmul(
            [hg], [(w_in_l, None, 0)], out_dtype=_BF16, row_ssq=ssq, bn=768, name="in_proj",
            side_casts=[(w_up, l, False), (w_out, l, False), (wq_x, l, False), (wo_x, l, False)])
        y_conv = _short_conv(proj, w_conv[l], width=conv_width)
        o_gla = _gla(proj, a_lr, w_gate_all[l], b_gate[l], gla_norm[l], conv_width=conv_width, dk=dk, dv=dv)
        h, hg, ssq, *w_in_next = _matmul(
            [y_conv, o_gla], [(w_out_l, None, 0), (w_out_l, None, conv_width)], out_dtype=_F32, res=h,
            next_gain=norm_xattn[l], bn=512, bk=conv_width, name="out_proj",
            side_casts=[(w_in_t, l + 1, True)] if l + 1 < depth else [])
        memn = _rmsnorm(mem2, norm_mem[l], _BF16, 256)
        k = _matmul([memn], [(wk_x, l, 0)], out_dtype=_BF16, bn=512, name="xattn_k")
        v = _matmul([memn], [(wv_x, l, 0)], out_dtype=_BF16, bn=512, name="xattn_v")
        h, hg, ssq = _xattn(hg, ssq, h, wq_l, wo_l, k, v, norm_mlp[l])
        hid, w_down_l = _matmul([hg], [(w_up_l, None, 0)], out_dtype=_BF16, act="relu2", row_ssq=ssq,
                                side_casts=[(w_down, l, False)], name="mlp_up")
        h = _matmul([hid], [(w_down_l, None, 0)], out_dtype=_F32, res=h, name="mlp_down")
        w_in_l = w_in_next[0] if w_in_next else None
    out = _rmsnorm(h, norm_final, _F32, 512)
    return out.reshape(bsz, seq, d)
```

```python
import functools
import math

import jax
import jax.numpy as jnp
import numpy as np
from jax import lax
from jax.experimental import pallas as pl
from jax.experimental.pallas import tpu as pltpu

EPS = 1e-6
GLA_CHUNK = 64
GLA_GROUP = 4
GLA_HEADS = 4
GLA_HEADS_PER_STEP = 4
XATTN_HEADS = 4
GATE_TAU = 16.0
CONV_K = 3
LANES = 128
SUBLANES = 8
MIB = 1024 * 1024
VMEM_CAP = 60 * MIB
VMEM_TEMPS = 12 * MIB
VMEM_DEFAULT = 40 * MIB
LOG2E = math.log2(math.e)

_BF16 = jnp.bfloat16
_F32 = jnp.float32
_NT = (((1,), (1,)), ((), ()))
_TN = (((0,), (0,)), ((), ()))


def _params(sem, block_bytes=None):
    limit = VMEM_DEFAULT if block_bytes is None else min(VMEM_CAP, block_bytes + VMEM_TEMPS)
    return pltpu.CompilerParams(dimension_semantics=sem, vmem_limit_bytes=limit)


def _lane_partial_ssq(h):
    sq = h * h
    acc = sq[:, :LANES]
    for t in range(1, h.shape[1] // LANES):
        acc = acc + sq[:, t * LANES:(t + 1) * LANES]
    return acc


def _row_rsqrt(ssq_ref, inv_d):
    return lax.rsqrt(jnp.sum(ssq_ref[...], axis=-1, keepdims=True) * inv_d + EPS)


def _rmsnorm_body(x_ref, g_ref, o_ref):
    x = x_ref[...]
    y = x * lax.rsqrt(jnp.mean(x * x, axis=-1, keepdims=True) + EPS)
    o_ref[...] = (y * g_ref[...]).astype(o_ref.dtype)


def _rmsnorm(x, gain, out_dtype, bm):
    m, d = x.shape
    bm = min(bm, m)
    return pl.pallas_call(
        _rmsnorm_body,
        out_shape=jax.ShapeDtypeStruct((m, d), out_dtype),
        grid=(m // bm,),
        in_specs=[pl.BlockSpec((bm, d), lambda i: (i, 0)), pl.BlockSpec((1, d), lambda i: (0, 0))],
        out_specs=pl.BlockSpec((bm, d), lambda i: (i, 0)),
        compiler_params=_params(("parallel",)),
        name="rmsnorm",
    )(x, gain.reshape(1, d).astype(_F32))


def _norm_prep_body(x_ref, g_ref, wa_ref, hg_ref, ssq_ref, a_ref, *, inv_d):
    x = x_ref[...]
    hg = (x * g_ref[...]).astype(hg_ref.dtype)
    hg_ref[...] = hg
    ssq = _lane_partial_ssq(x)
    ssq_ref[...] = ssq
    r = lax.rsqrt(jnp.sum(ssq, axis=-1, keepdims=True) * inv_d + EPS)
    a_ref[...] = jnp.dot(hg, wa_ref[...], preferred_element_type=_F32) * r


def _norm_prep(x, gain, w_a, bm=512):
    m, d = x.shape
    na = w_a.shape[1]
    row_block = lambda width: pl.BlockSpec((bm, width), lambda i: (i, 0))
    return pl.pallas_call(
        functools.partial(_norm_prep_body, inv_d=1.0 / d),
        out_shape=(jax.ShapeDtypeStruct((m, d), _BF16), jax.ShapeDtypeStruct((m, LANES), _F32),
                   jax.ShapeDtypeStruct((m, na), _F32)),
        grid=(m // bm,),
        in_specs=[row_block(d), pl.BlockSpec((1, d), lambda i: (0, 0)),
                  pl.BlockSpec((d, na), lambda i: (0, 0))],
        out_specs=(row_block(d), row_block(LANES), row_block(na)),
        compiler_params=_params(("parallel",)),
        name="norm_prep",
    )(x, gain.reshape(1, d).astype(_F32), w_a)


def _mm_body(*refs, npairs, grid, act, has_res, inv_d, cast_w, sides):
    it = iter(refs)
    xs = [next(it) for _ in range(npairs)]
    ws = [next(it) for _ in range(npairs)]
    res_ref = next(it) if has_res else None
    ssq_in_ref = next(it) if inv_d is not None else None
    side_src = [next(it) for _ in sides]
    o_ref = next(it)
    side_dst = [next(it) for _ in sides]
    nk = grid[2]

    step = (pl.program_id(0) * grid[1] + pl.program_id(1)) * nk + pl.program_id(2)
    for (transposed, nblocks), src, dst in zip(sides, side_src, side_dst):
        def cast(src=src, dst=dst, transposed=transposed):
            val = src[...]
            dst[...] = (val.T if transposed else val).astype(dst.dtype)

        if nblocks == grid[0] * grid[1] * nk:
            cast()
        else:
            pl.when(step < nblocks)(cast)

    def weight(w_ref):
        return w_ref[...].astype(_BF16) if cast_w else w_ref[...]

    acc = jnp.dot(xs[0][...], weight(ws[0]), preferred_element_type=_F32)
    for p in range(1, npairs):
        acc = acc + jnp.dot(xs[p][...], weight(ws[p]), preferred_element_type=_F32)

    if nk > 1:
        @pl.when(pl.program_id(2) == 0)
        def _():
            o_ref[...] = res_ref[...] if has_res else jnp.zeros_like(o_ref)

        o_ref[...] += acc
        return

    if inv_d is not None:
        acc = acc * _row_rsqrt(ssq_in_ref, inv_d)
    if act == "relu2":
        r = jnp.maximum(acc, 0.0)
        acc = r * r
    if has_res:
        acc = acc + res_ref[...]
    o_ref[...] = acc.astype(o_ref.dtype)


def _matmul(xs, ws, *, out_dtype, res=None, act=None, row_ssq=None, side_casts=(),
            bm=1024, bn=1024, bk=4096, name="matmul"):
    cast_w = ws[0][0].dtype == _F32
    w_bytes = 4 if cast_w else 2
    npairs = len(xs)
    m, kdim = xs[0].shape
    n = ws[0][0].shape[-1]
    bm, bn, bk = min(bm, m), min(bn, n), min(bk, kdim)
    nk = kdim // bk
    assert m % bm == 0 and kdim % bk == 0
    assert nk == 1 or (act is None and row_ssq is None and out_dtype == _F32)
    grid = (m // bm, n // bn, nk)
    nsteps = grid[0] * grid[1] * nk
    in_specs = [pl.BlockSpec((bm, bk), lambda i, j, k: (i, k))] * npairs
    for w, layer, row0 in ws:
        assert row0 % bk == 0
        kb = row0 // bk
        if layer is None:
            in_specs.append(pl.BlockSpec((bk, bn), lambda i, j, k, kb=kb: (kb + k, j)))
        else:
            in_specs.append(pl.BlockSpec((None, bk, bn), lambda i, j, k, kb=kb, layer=layer: (layer, kb + k, j)))
    args = list(xs) + [w for w, _, _ in ws]
    out_bytes = jnp.dtype(out_dtype).itemsize
    block_bytes = 2 * (npairs * (2 * bm * bk + w_bytes * bk * bn) + bm * bn * out_bytes)
    if res is not None:
        in_specs.append(pl.BlockSpec((bm, bn), lambda i, j, k: (i, j)))
        args.append(res)
        block_bytes += 2 * 4 * bm * bn
    inv_d = None
    if row_ssq is not None:
        in_specs.append(pl.BlockSpec((bm, LANES), lambda i, j, k: (i, 0)))
        args.append(row_ssq)
        inv_d = 1.0 / kdim
        block_bytes += 2 * 4 * bm * LANES
    out_shape = [jax.ShapeDtypeStruct((m, grid[1] * bn), out_dtype)]
    out_specs = [pl.BlockSpec((bm, bn), lambda i, j, k: (i, j))]
    sides = []
    for src, layer, transposed in side_casts:
        _, rows, cols = src.shape
        if transposed:
            slab = LANES * pl.cdiv(pl.cdiv(rows, LANES), nsteps)
            nblocks = pl.cdiv(rows, slab)
            out_shape.append(jax.ShapeDtypeStruct((cols, rows), _BF16))
        else:
            tile = 2 * SUBLANES
            slab = next(c for c in range(tile, rows + 1, tile) if rows % c == 0 and rows // c <= nsteps)
            nblocks = rows // slab
            out_shape.append(jax.ShapeDtypeStruct((rows, cols), _BF16))
        assert nblocks <= nsteps
        blk = lambda i, j, k, last=nblocks - 1: jnp.minimum((i * grid[1] + j) * nk + k, last)
        in_specs.append(pl.BlockSpec((None, slab, cols), lambda i, j, k, layer=layer, blk=blk: (layer, blk(i, j, k), 0)))
        args.append(src)
        if transposed:
            out_specs.append(pl.BlockSpec((cols, slab), lambda i, j, k, blk=blk: (0, blk(i, j, k))))
        else:
            out_specs.append(pl.BlockSpec((slab, cols), lambda i, j, k, blk=blk: (blk(i, j, k), 0)))
        sides.append((transposed, nblocks))
        block_bytes += 2 * (4 + 2) * slab * cols
    body = functools.partial(_mm_body, npairs=npairs, grid=grid, act=act, has_res=res is not None, inv_d=inv_d,
                             cast_w=cast_w, sides=tuple(sides))
    outs = pl.pallas_call(
        body,
        out_shape=tuple(out_shape),
        grid=grid,
        in_specs=in_specs,
        out_specs=tuple(out_specs),
        compiler_params=_params(("parallel", "arbitrary", "arbitrary"), block_bytes),
        name=name,
    )(*args)
    return outs[0] if len(outs) == 1 else outs


def _cast_transposed_body(src_ref, dst_ref):
    dst_ref[...] = src_ref[...].T.astype(dst_ref.dtype)


def _cast_transposed(w_t, layer, bn=256):
    _, n, kdim = w_t.shape
    return pl.pallas_call(
        _cast_transposed_body,
        out_shape=jax.ShapeDtypeStruct((kdim, n), _BF16),
        grid=(pl.cdiv(n, bn),),
        in_specs=[pl.BlockSpec((None, bn, kdim), lambda t: (layer, t, 0))],
        out_specs=pl.BlockSpec((kdim, bn), lambda t: (0, t)),
        compiler_params=_params(("parallel",)),
        name="cast_transposed",
    )(w_t)


def _conv_body(b_ref, c_ref, h_ref, w_ref, o_ref, tail_ref):
    t = pl.program_id(1)

    @pl.when(t == 0)
    def _():
        tail_ref[...] = jnp.zeros_like(tail_ref)

    u = c_ref[...].astype(_F32) * h_ref[...].astype(_F32)
    rows = u.shape[0]
    row = lax.broadcasted_iota(jnp.int32, u.shape, 0)
    prev1 = tail_ref[SUBLANES - 1:SUBLANES, :]
    prev2 = tail_ref[SUBLANES - 2:SUBLANES - 1, :]
    u1 = jnp.where(row == 0, prev1, pltpu.roll(u, 1, axis=0))
    u2 = jnp.where(row == 0, prev2, jnp.where(row == 1, prev1, pltpu.roll(u, 2, axis=0)))
    w = w_ref[...]
    conv = w[0:1, :] * u2 + w[1:2, :] * u1 + w[2:3, :] * u
    o_ref[...] = (b_ref[...].astype(_F32) * conv).astype(o_ref.dtype)
    tail_ref[...] = u[rows - SUBLANES:, :]


def _short_conv(proj, w_conv, *, width, tb=2048, cb=512):
    s = proj.shape[0]
    tb = min(tb, s)
    ncb = width // cb
    w_pad = jnp.zeros((SUBLANES, width), _F32).at[:CONV_K].set(w_conv.astype(_F32))
    return pl.pallas_call(
        _conv_body,
        out_shape=jax.ShapeDtypeStruct((s, width), _BF16),
        grid=(ncb, s // tb),
        in_specs=[pl.BlockSpec((tb, cb), lambda c, t: (t, c)),
                  pl.BlockSpec((tb, cb), lambda c, t: (t, ncb + c)),
                  pl.BlockSpec((tb, cb), lambda c, t: (t, 2 * ncb + c)),
                  pl.BlockSpec((SUBLANES, cb), lambda c, t: (0, c))],
        out_specs=pl.BlockSpec((tb, cb), lambda c, t: (t, c)),
        scratch_shapes=[pltpu.VMEM((SUBLANES, cb), _F32)],
        compiler_params=_params(("parallel", "arbitrary")),
        name="short_conv",
    )(proj, proj, proj, w_pad)


def _bf16_split_dot(a_exact_bf16, x):
    hi = x.astype(_BF16)
    lo = (x - hi.astype(_F32)).astype(_BF16)
    return (jnp.dot(a_exact_bf16, hi, preferred_element_type=_F32)
            + jnp.dot(a_exact_bf16, lo, preferred_element_type=_F32))


def _gla_constants(L, G):
    nlev = L.bit_length() - 1
    r = np.arange(L * G)
    cum_op = (r[None, :] <= r[:, None]) & (r[None, :] // L == r[:, None] // L)
    i = np.arange(L)
    base = [np.eye(L, dtype=bool)]
    for lev in range(nlev):
        s = L >> lev
        lower = (i % s) >= s // 2
        base.append((i[:, None] // s == i[None, :] // s) & lower[:, None] & ~lower[None, :])
    base = np.stack(base).astype(np.float32)
    masks = np.zeros((nlev + 1, 2, L, 2 * L), np.float32)
    masks[:, 0, :, :L] = base
    masks[:, 1, :, L:] = base
    return jnp.asarray(cum_op, _BF16), jnp.asarray(masks)


def _gla_level_operand(q, k, b, s, row8):
    half = s // 2
    n = b.shape[0]
    parts = []
    if half >= SUBLANES:
        for r0 in range(0, n, half):
            rows = slice(r0, r0 + half)
            r_bound = (r0 // s) * s + half - 1
            b_r = b[r_bound:r_bound + 1, :]
            if (r0 // half) % 2:
                parts.append(q[rows] * jnp.exp2(b[rows] - b_r))
            else:
                parts.append(k[rows] * jnp.exp2(b_r - b[rows]))
    else:
        low = (row8 & (s - 1)) >= half
        sign = jnp.where(low, -1.0, 1.0)
        for r0 in range(0, n, SUBLANES):
            rows = slice(r0, r0 + SUBLANES)
            b8 = b[rows]
            if half == 1:
                b_r = jnp.where(low, pltpu.roll(b8, 1, axis=0), b8)
            else:
                b_r = b8[half - 1:half, :]
                for blk in range(1, SUBLANES // s):
                    b_r = jnp.where(row8 >= blk * s, b8[blk * s + half - 1:blk * s + half, :], b_r)
            parts.append(jnp.where(low, q[rows], k[rows]) * jnp.exp2((b_r - b8) * sign))
    return jnp.concatenate(parts, axis=0).astype(_BF16)


def _gla_prepare_head(q, k, v, la, cum_op, mask_ref, row8):
    L, G = GLA_CHUNK, GLA_GROUP
    nlev = L.bit_length() - 1
    dk = q.shape[1]
    b = _bf16_split_dot(cum_op, la)
    b_last = [b[(c + 1) * L - 1:(c + 1) * L, :] for c in range(G)]
    q_in = (q * jnp.exp2(b)).astype(_BF16)
    k_dec = jnp.concatenate(
        [k[c * L:(c + 1) * L] * jnp.exp2(b_last[c] - b[c * L:(c + 1) * L]) for c in range(G)], axis=0
    ).astype(_BF16)
    decays = [jnp.exp2(bl) for bl in b_last]

    acc = [None] * G

    def add_level(idx, p):
        for c in range(G):
            t = c // 2
            blk = p[c * L:(c + 1) * L, t * LANES:(t + 1) * LANES] * mask_ref[idx, c % 2]
            acc[c] = blk if acc[c] is None else acc[c] + blk

    add_level(0, lax.dot_general(q.astype(_BF16), k.astype(_BF16), _NT, preferred_element_type=_F32))
    for lev in range(nlev):
        x = _gla_level_operand(q, k, b, L >> lev, row8)
        add_level(lev + 1, lax.dot_general(x, x, _NT, preferred_element_type=_F32))

    zeros = jnp.zeros((L, LANES), _F32)
    a_bd = jnp.concatenate(
        [jnp.concatenate([acc[c] if t == c // 2 else zeros for t in range(G // 2)], axis=1) for c in range(G)],
        axis=0).astype(_BF16)
    o_intra = jnp.dot(a_bd, v, preferred_element_type=_F32)
    return o_intra, q_in, k_dec, decays


def _gla_body(q_ref, k_ref, v_ref, g_ref, a_ref, wg_ref, bg_ref, gn_ref, cum_ref, mask_ref, o_ref,
              state_ref, oin_ref, qin_ref, kdec_ref, dec_ref, *, dk, dv):
    L, G = GLA_CHUNK, GLA_GROUP

    @pl.when(pl.program_id(1) == 0)
    def _():
        state_ref[...] = jnp.zeros_like(state_ref)

    z = jnp.dot(a_ref[...].astype(_BF16), wg_ref[...], preferred_element_type=_F32) + bg_ref[...]
    la_all = (jnp.minimum(z, 0.0) - jnp.log(1.0 + jnp.exp(-jnp.abs(z)))) * (LOG2E / GATE_TAU)

    row8 = lax.broadcasted_iota(jnp.int32, (SUBLANES, dk), 0)
    scale = dk ** -0.5
    cum_op = cum_ref[...]
    for hd in range(GLA_HEADS_PER_STEP):
        kc = slice(hd * dk, (hd + 1) * dk)
        vc = slice(hd * dv, (hd + 1) * dv)
        o_intra, q_in, k_dec, decays = _gla_prepare_head(
            q_ref[:, kc].astype(_F32) * scale, k_ref[:, kc].astype(_F32), v_ref[:, vc], la_all[:, kc],
            cum_op, mask_ref, row8)
        oin_ref[:, vc] = o_intra
        qin_ref[:, kc] = q_in
        kdec_ref[:, kc] = k_dec
        for c in range(G):
            dec_ref[c:c + 1, kc] = decays[c]

    for c in range(G):
        rows = slice(c * L, (c + 1) * L)
        for hd in range(GLA_HEADS_PER_STEP):
            kc = slice(hd * dk, (hd + 1) * dk)
            vc = slice(hd * dv, (hd + 1) * dv)
            st = state_ref[hd]
            o = oin_ref[rows, vc] + lax.dot_general(qin_ref[rows, kc], st.astype(_BF16), _NT,
                                                    preferred_element_type=_F32)
            upd = lax.dot_general(v_ref[rows, vc], kdec_ref[rows, kc], _TN, preferred_element_type=_F32)
            state_ref[hd] = st * dec_ref[c:c + 1, kc] + upd
            o = o * lax.rsqrt(jnp.mean(o * o, axis=-1, keepdims=True) + EPS)
            g = g_ref[rows, vc].astype(_F32)
            o_ref[rows, vc] = (o * gn_ref[:, vc] * (g * jax.nn.sigmoid(g))).astype(o_ref.dtype)


def _gla(proj, a_lr, w_gate_pad, b_gate, gla_norm, *, conv_width, dk, dv):
    s = proj.shape[0]
    hp = GLA_HEADS_PER_STEP
    tb = GLA_CHUNK * GLA_GROUP
    kw, vw = hp * dk, hp * dv
    q_blk = (3 * conv_width) // kw
    k_blk = q_blk + GLA_HEADS // hp
    v_blk = (3 * conv_width + 2 * GLA_HEADS * dk) // vw
    g_blk = v_blk + GLA_HEADS // hp
    assert q_blk * kw == 3 * conv_width and v_blk * vw == 3 * conv_width + 2 * GLA_HEADS * dk
    assert 2 * GLA_CHUNK == LANES and GLA_GROUP % 2 == 0 and s % tb == 0 and GLA_HEADS % hp == 0
    na = a_lr.shape[1]
    cum_op, pair_mask = _gla_constants(GLA_CHUNK, GLA_GROUP)
    body = functools.partial(_gla_body, dk=dk, dv=dv)
    return pl.pallas_call(
        body,
        out_shape=jax.ShapeDtypeStruct((s, GLA_HEADS * dv), _BF16),
        grid=(GLA_HEADS // hp, s // tb),
        in_specs=[pl.BlockSpec((tb, kw), lambda hg, t: (t, q_blk + hg)),
                  pl.BlockSpec((tb, kw), lambda hg, t: (t, k_blk + hg)),
                  pl.BlockSpec((tb, vw), lambda hg, t: (t, v_blk + hg)),
                  pl.BlockSpec((tb, vw), lambda hg, t: (t, g_blk + hg)),
                  pl.BlockSpec((tb, na), lambda hg, t: (t, 0)),
                  pl.BlockSpec((na, kw), lambda hg, t: (0, hg)),
                  pl.BlockSpec((1, kw), lambda hg, t: (0, hg)),
                  pl.BlockSpec((1, vw), lambda hg, t: (0, hg)),
                  pl.BlockSpec(cum_op.shape, lambda hg, t: (0, 0)),
                  pl.BlockSpec(pair_mask.shape, lambda hg, t: (0, 0, 0, 0))],
        out_specs=pl.BlockSpec((tb, vw), lambda hg, t: (t, hg)),
        scratch_shapes=[pltpu.VMEM((hp, dv, dk), _F32),
                        pltpu.VMEM((tb, vw), _F32),
                        pltpu.VMEM((tb, kw), _BF16),
                        pltpu.VMEM((tb, kw), _BF16),
                        pltpu.VMEM((SUBLANES, kw), _F32)],
        compiler_params=_params(("parallel", "arbitrary")),
        name="gla",
    )(proj, proj, proj, proj, a_lr, w_gate_pad, b_gate.reshape(1, -1).astype(_F32),
      gla_norm.reshape(1, -1).astype(_F32), cum_op, pair_mask)


def _xattn_body(res_ref, gin_ref, wq_ref, k_ref, v_ref, wo_ref, gain_ref, o_ref, hgo_ref, ssqo_ref, *, dh):
    scale = dh ** -0.5
    x = res_ref[...]
    r = lax.rsqrt(jnp.mean(x * x, axis=-1, keepdims=True) + EPS)
    q = jnp.dot((x * gin_ref[...]).astype(_BF16), wq_ref[...], preferred_element_type=_F32) * r
    q = q.astype(_BF16)
    heads = []
    for hd in range(XATTN_HEADS):
        sl = slice(hd * dh, (hd + 1) * dh)
        sc = lax.dot_general(q[:, sl], k_ref[:, sl], _NT, preferred_element_type=_F32) * scale
        p = jnp.exp(sc - jnp.max(sc, axis=-1, keepdims=True))
        denom = jnp.sum(p, axis=-1, keepdims=True)
        pv = jnp.dot(p.astype(_BF16), v_ref[:, sl], preferred_element_type=_F32)
        heads.append((pv / denom).astype(_BF16))
    att = jnp.concatenate(heads, axis=1)
    h = jnp.dot(att, wo_ref[...], preferred_element_type=_F32) + x
    o_ref[...] = h
    hgo_ref[...] = (h * gain_ref[...]).astype(hgo_ref.dtype)
    ssqo_ref[...] = _lane_partial_ssq(h)


def _xattn(res, gain, wq, wo, k, v, next_gain, *, bm=256):
    s, d = res.shape
    mlen, width = k.shape
    body = functools.partial(_xattn_body, dh=width // XATTN_HEADS)
    row_block = lambda w: pl.BlockSpec((bm, w), lambda i: (i, 0))
    whole = lambda a: pl.BlockSpec(a.shape, lambda i: (0, 0))
    w_block = lambda a: pl.BlockSpec(a.shape, lambda i: (0, 0), pipeline_mode=pl.Buffered(1))
    gain_row = pl.BlockSpec((1, d), lambda i: (0, 0))
    block_bytes = (2 * bm * 4 * d + 2 * bm * (4 * d + 2 * d + 4 * LANES)
                   + 2 * 2 * d * width + 2 * 2 * 2 * mlen * width)
    return pl.pallas_call(
        body,
        out_shape=(jax.ShapeDtypeStruct((s, d), _F32), jax.ShapeDtypeStruct((s, d), _BF16),
                   jax.ShapeDtypeStruct((s, LANES), _F32)),
        grid=(s // bm,),
        in_specs=[row_block(d), gain_row, w_block(wq), whole(k), whole(v), w_block(wo), gain_row],
        out_specs=(row_block(d), row_block(d), row_block(LANES)),
        compiler_params=_params(("parallel",), block_bytes),
        name="xattn",
    )(res, gain.reshape(1, d).astype(_F32), wq, k, v, wo, next_gain.reshape(1, d).astype(_F32))


def kernel(x, mem, norm_mix, w_in, w_conv, w_gate_up, b_gate, gla_norm, w_out, norm_xattn, norm_mem,
           wq_x, wk_x, wv_x, wo_x, norm_mlp, w_up, w_down, norm_final):
    bsz, seq, d = x.shape
    depth = w_in.shape[0]
    conv_width = w_conv.shape[2]
    gla_key = w_gate_up.shape[2]
    gla_width = gla_norm.shape[1]
    gate_rank = w_gate_up.shape[1]
    dk = gla_key // GLA_HEADS
    dv = gla_width // GLA_HEADS
    main_cols = 3 * conv_width + 2 * gla_key + 2 * gla_width
    assert bsz == 1 and w_in.shape[2] == main_cols + gate_rank

    w_gate_all = jnp.zeros((depth, LANES, gla_key), _BF16).at[:, :gate_rank].set(w_gate_up.astype(_BF16))
    w_in_t = jnp.swapaxes(w_in, 1, 2)
    w_in_l = _cast_transposed(w_in_t, 0)

    h = x.reshape(seq, d)
    mem2 = mem.reshape(mem.shape[1], d)
    for l in range(depth):
        w_a = jnp.zeros((d, LANES), _BF16).at[:, :gate_rank].set(w_in_l[:, main_cols:])
        hg, ssq, a_lr = _norm_prep(h, norm_mix[l], w_a)
        proj, w_up_l, w_out_l, wq_l, wo_l = _matmul(
            [hg], [(w_in_l, None, 0)], out_dtype=_BF16, row_ssq=ssq, name="in_proj",
            side_casts=[(w_up, l, False), (w_out, l, False), (wq_x, l, False), (wo_x, l, False)])
        y_conv = _short_conv(proj, w_conv[l], width=conv_width)
        o_gla = _gla(proj, a_lr, w_gate_all[l], b_gate[l], gla_norm[l], conv_width=conv_width, dk=dk, dv=dv)
        h = _matmul([y_conv, o_gla], [(w_out_l, None, 0), (w_out_l, None, conv_width)], out_dtype=_F32, res=h,
                    bk=conv_width, name="out_proj")
        memn = _rmsnorm(mem2, norm_mem[l], _BF16, 256)
        k = _matmul([memn], [(wk_x, l, 0)], out_dtype=_BF16, bn=512, name="xattn_k")
        v = _matmul([memn], [(wv_x, l, 0)], out_dtype=_BF16, bn=512, name="xattn_v")
        h, hg, ssq = _xattn(h, norm_xattn[l], wq_l, wo_l, k, v, norm_mlp[l])
        hid, w_down_l, *w_in_next = _matmul(
            [hg], [(w_up_l, None, 0)], out_dtype=_BF16, act="relu2", row_ssq=ssq, name="mlp_up",
            side_casts=[(w_down, l, False)] + ([(w_in_t, l + 1, True)] if l + 1 < depth else []))
        h = _matmul([hid], [(w_down_l, None, 0)], out_dtype=_F32, res=h, name="mlp_down")
        w_in_l = w_in_next[0] if w_in_next else None
    out = _rmsnorm(h, norm_final, _F32, 512)
    return out.reshape(bsz, seq, d)
```

```python
import functools
import math

import jax
import jax.numpy as jnp
import numpy as np
from jax import lax
from jax.experimental import pallas as pl
from jax.experimental.pallas import tpu as pltpu

EPS = 1e-6
GLA_CHUNK = 64
GLA_GROUP = 4
GLA_HEADS = 4
GLA_HEADS_PER_STEP = 4
XATTN_HEADS = 4
GATE_TAU = 16.0
CONV_K = 3
CONV_SEG = 256
LANES = 128
SUBLANES = 8
MIB = 1024 * 1024
VMEM_CAP = 60 * MIB
VMEM_TEMPS = 12 * MIB
VMEM_DEFAULT = 40 * MIB
LOG2E = math.log2(math.e)

_BF16 = jnp.bfloat16
_F32 = jnp.float32
_NT = (((1,), (1,)), ((), ()))
_TN = (((0,), (0,)), ((), ()))


def _params(sem, block_bytes=None):
    limit = VMEM_DEFAULT if block_bytes is None else min(VMEM_CAP, block_bytes + VMEM_TEMPS)
    return pltpu.CompilerParams(dimension_semantics=sem, vmem_limit_bytes=limit)


def _lane_partial_ssq(h):
    sq = h * h
    acc = sq[:, :LANES]
    for t in range(1, h.shape[1] // LANES):
        acc = acc + sq[:, t * LANES:(t + 1) * LANES]
    return acc


def _row_rsqrt(ssq_ref, inv_d):
    return lax.rsqrt(jnp.sum(ssq_ref[...], axis=-1, keepdims=True) * inv_d + EPS)


def _rmsnorm_body(x_ref, g_ref, o_ref):
    x = x_ref[...]
    y = x * lax.rsqrt(jnp.mean(x * x, axis=-1, keepdims=True) + EPS)
    o_ref[...] = (y * g_ref[...]).astype(o_ref.dtype)


def _rmsnorm(x, gain, out_dtype, bm):
    m, d = x.shape
    bm = min(bm, m)
    return pl.pallas_call(
        _rmsnorm_body,
        out_shape=jax.ShapeDtypeStruct((m, d), out_dtype),
        grid=(m // bm,),
        in_specs=[pl.BlockSpec((bm, d), lambda i: (i, 0)), pl.BlockSpec((1, d), lambda i: (0, 0))],
        out_specs=pl.BlockSpec((bm, d), lambda i: (i, 0)),
        compiler_params=_params(("parallel",)),
        name="rmsnorm",
    )(x, gain.reshape(1, d).astype(_F32))


def _norm_prep_body(x_ref, g_ref, wa_ref, hg_ref, ssq_ref, a_ref, *, inv_d):
    x = x_ref[...]
    hg = (x * g_ref[...]).astype(hg_ref.dtype)
    hg_ref[...] = hg
    ssq = _lane_partial_ssq(x)
    ssq_ref[...] = ssq
    r = lax.rsqrt(jnp.sum(ssq, axis=-1, keepdims=True) * inv_d + EPS)
    a_ref[...] = jnp.dot(hg, wa_ref[...], preferred_element_type=_F32) * r


def _norm_prep(x, gain, w_a, bm=512):
    m, d = x.shape
    na = w_a.shape[1]
    row_block = lambda width: pl.BlockSpec((bm, width), lambda i: (i, 0))
    return pl.pallas_call(
        functools.partial(_norm_prep_body, inv_d=1.0 / d),
        out_shape=(jax.ShapeDtypeStruct((m, d), _BF16), jax.ShapeDtypeStruct((m, LANES), _F32),
                   jax.ShapeDtypeStruct((m, na), _F32)),
        grid=(m // bm,),
        in_specs=[row_block(d), pl.BlockSpec((1, d), lambda i: (0, 0)),
                  pl.BlockSpec((d, na), lambda i: (0, 0))],
        out_specs=(row_block(d), row_block(LANES), row_block(na)),
        compiler_params=_params(("parallel",)),
        name="norm_prep",
    )(x, gain.reshape(1, d).astype(_F32), w_a)


def _mm_body(*refs, npairs, grid, act, has_res, inv_d, cast_w, sides, has_conv):
    it = iter(refs)
    xs = [next(it) for _ in range(npairs)]
    ws = [next(it) for _ in range(npairs)]
    res_ref = next(it) if has_res else None
    ssq_in_ref = next(it) if inv_d is not None else None
    wconv_ref = next(it) if has_conv else None
    side_src = [next(it) for _ in sides]
    o_ref = next(it)
    side_dst = [next(it) for _ in sides]
    tail_ref = next(it) if has_conv else None
    nk = grid[2]

    step = (pl.program_id(0) * grid[1] + pl.program_id(1)) * nk + pl.program_id(2)
    for (transposed, nblocks), src, dst in zip(sides, side_src, side_dst):
        def cast(src=src, dst=dst, transposed=transposed):
            val = src[...]
            dst[...] = (val.T if transposed else val).astype(dst.dtype)

        if nblocks == grid[0] * grid[1] * nk:
            cast()
        else:
            pl.when(step < nblocks)(cast)

    def weight(w_ref):
        return w_ref[...].astype(_BF16) if cast_w else w_ref[...]

    acc = jnp.dot(xs[0][...], weight(ws[0]), preferred_element_type=_F32)
    for p in range(1, npairs):
        acc = acc + jnp.dot(xs[p][...], weight(ws[p]), preferred_element_type=_F32)

    if nk > 1:
        @pl.when(pl.program_id(2) == 0)
        def _():
            o_ref[...] = res_ref[...] if has_res else jnp.zeros_like(o_ref)

        o_ref[...] += acc
        return

    if inv_d is not None:
        acc = acc * _row_rsqrt(ssq_in_ref, inv_d)
    if act == "relu2":
        r = jnp.maximum(acc, 0.0)
        acc = r * r
    if has_res:
        acc = acc + res_ref[...]
    if not has_conv:
        o_ref[...] = acc.astype(o_ref.dtype)
        return

    j = pl.program_id(1)

    @pl.when(step == 0)
    def _():
        tail_ref[...] = jnp.zeros_like(tail_ref)

    seg = acc.shape[1] // 3
    u = acc[:, seg:2 * seg] * acc[:, 2 * seg:]
    rows = u.shape[0]
    row = lax.broadcasted_iota(jnp.int32, u.shape, 0)
    tail = tail_ref[j]
    prev1 = tail[SUBLANES - 1:SUBLANES, :]
    prev2 = tail[SUBLANES - 2:SUBLANES - 1, :]
    u1 = jnp.where(row == 0, prev1, pltpu.roll(u, 1, axis=0))
    u2 = jnp.where(row == 0, prev2, jnp.where(row == 1, prev1, pltpu.roll(u, 2, axis=0)))
    w = wconv_ref[...]
    y = acc[:, :seg] * (w[0:1, :] * u2 + w[1:2, :] * u1 + w[2:3, :] * u)
    o_ref[...] = y.astype(o_ref.dtype)
    tail_ref[j] = u[rows - SUBLANES:, :]


def _conv_group_order(blk, nseg):
    return jnp.where(blk < 3 * nseg, 3 * (blk % nseg) + blk // nseg, blk)


def _matmul(xs, ws, *, out_dtype, res=None, act=None, row_ssq=None, side_casts=(), conv=None, cols=None,
            bm=1024, bn=1024, bk=4096, name="matmul"):
    cast_w = ws[0][0].dtype == _F32
    w_bytes = 4 if cast_w else 2
    npairs = len(xs)
    m, kdim = xs[0].shape
    n = ws[0][0].shape[-1]
    bm, bn, bk = min(bm, m), min(bn, n), min(bk, kdim)
    nk = kdim // bk
    assert m % bm == 0 and kdim % bk == 0
    assert nk == 1 or (act is None and row_ssq is None and out_dtype == _F32)
    j0, ntiles = cols if cols is not None else (0, n // bn)
    grid = (m // bm, ntiles, nk)
    nsteps = grid[0] * grid[1] * nk
    in_specs = [pl.BlockSpec((bm, bk), lambda i, j, k: (i, k))] * npairs
    for w, layer, row0 in ws:
        assert row0 % bk == 0
        kb = row0 // bk
        if layer is None:
            in_specs.append(pl.BlockSpec((bk, bn), lambda i, j, k, kb=kb: (kb + k, j0 + j)))
        else:
            in_specs.append(pl.BlockSpec((None, bk, bn), lambda i, j, k, kb=kb, layer=layer: (layer, kb + k, j0 + j)))
    args = list(xs) + [w for w, _, _ in ws]
    out_bytes = jnp.dtype(out_dtype).itemsize
    block_bytes = 2 * (npairs * (2 * bm * bk + w_bytes * bk * bn) + bm * bn * out_bytes)
    if res is not None:
        in_specs.append(pl.BlockSpec((bm, bn), lambda i, j, k: (i, j)))
        args.append(res)
        block_bytes += 2 * 4 * bm * bn
    inv_d = None
    if row_ssq is not None:
        in_specs.append(pl.BlockSpec((bm, LANES), lambda i, j, k: (i, 0)))
        args.append(row_ssq)
        inv_d = 1.0 / kdim
        block_bytes += 2 * 4 * bm * LANES
    scratch = []
    if conv is None:
        out_shape = [jax.ShapeDtypeStruct((m, ntiles * bn), out_dtype)]
        out_specs = [pl.BlockSpec((bm, bn), lambda i, j, k: (i, j))]
    else:
        seg = bn // 3
        assert seg == CONV_SEG and nk == 1 and conv.shape[1] == ntiles * seg
        in_specs.append(pl.BlockSpec((SUBLANES, seg), lambda i, j, k: (0, j)))
        args.append(conv)
        out_shape = [jax.ShapeDtypeStruct((m, ntiles * seg), out_dtype)]
        out_specs = [pl.BlockSpec((bm, seg), lambda i, j, k: (i, j))]
        scratch = [pltpu.VMEM((ntiles, SUBLANES, seg), _F32)]
    sides = []
    for src, layer, conv_groups in side_casts:
        _, rows, cols = src.shape
        transposed = conv_groups is not None
        if transposed:
            slab = CONV_SEG
            nblocks = pl.cdiv(rows, slab)
            out_shape.append(jax.ShapeDtypeStruct((cols, rows), _BF16))
        else:
            tile = 2 * SUBLANES
            slab = next(c for c in range(tile, rows + 1, tile) if rows % c == 0 and rows // c <= nsteps)
            nblocks = rows // slab
            out_shape.append(jax.ShapeDtypeStruct((rows, cols), _BF16))
        assert nblocks <= nsteps
        blk = lambda i, j, k, last=nblocks - 1: jnp.minimum((i * grid[1] + j) * nk + k, last)
        in_specs.append(pl.BlockSpec((None, slab, cols), lambda i, j, k, layer=layer, blk=blk: (layer, blk(i, j, k), 0)))
        args.append(src)
        if transposed:
            out_specs.append(pl.BlockSpec(
                (cols, slab), lambda i, j, k, blk=blk, g=conv_groups: (0, _conv_group_order(blk(i, j, k), g))))
        else:
            out_specs.append(pl.BlockSpec((slab, cols), lambda i, j, k, blk=blk: (blk(i, j, k), 0)))
        sides.append((transposed, nblocks))
        block_bytes += 2 * (4 + 2) * slab * cols
    body = functools.partial(_mm_body, npairs=npairs, grid=grid, act=act, has_res=res is not None, inv_d=inv_d,
                             cast_w=cast_w, sides=tuple(sides), has_conv=conv is not None)
    outs = pl.pallas_call(
        body,
        out_shape=tuple(out_shape),
        grid=grid,
        in_specs=in_specs,
        out_specs=tuple(out_specs),
        scratch_shapes=scratch,
        compiler_params=_params(("arbitrary", "arbitrary", "arbitrary"), block_bytes),
        name=name,
    )(*args)
    return outs[0] if len(outs) == 1 else outs


def _cast_transposed_body(src_ref, dst_ref):
    dst_ref[...] = src_ref[...].T.astype(dst_ref.dtype)


def _cast_transposed(w_t, layer, conv_groups):
    _, n, kdim = w_t.shape
    bn = CONV_SEG
    return pl.pallas_call(
        _cast_transposed_body,
        out_shape=jax.ShapeDtypeStruct((kdim, n), _BF16),
        grid=(pl.cdiv(n, bn),),
        in_specs=[pl.BlockSpec((None, bn, kdim), lambda t: (layer, t, 0))],
        out_specs=pl.BlockSpec((kdim, bn), lambda t: (0, _conv_group_order(t, conv_groups))),
        compiler_params=_params(("parallel",)),
        name="cast_transposed",
    )(w_t)


def _bf16_split_dot(a_exact_bf16, x):
    hi = x.astype(_BF16)
    lo = (x - hi.astype(_F32)).astype(_BF16)
    return (jnp.dot(a_exact_bf16, hi, preferred_element_type=_F32)
            + jnp.dot(a_exact_bf16, lo, preferred_element_type=_F32))


def _gla_constants(L, G):
    nlev = L.bit_length() - 1
    r = np.arange(L * G)
    cum_op = (r[None, :] <= r[:, None]) & (r[None, :] // L == r[:, None] // L)
    i = np.arange(L)
    base = [np.eye(L, dtype=bool)]
    for lev in range(nlev):
        s = L >> lev
        lower = (i % s) >= s // 2
        base.append((i[:, None] // s == i[None, :] // s) & lower[:, None] & ~lower[None, :])
    base = np.stack(base).astype(np.float32)
    masks = np.zeros((nlev + 1, 2, L, 2 * L), np.float32)
    masks[:, 0, :, :L] = base
    masks[:, 1, :, L:] = base
    return jnp.asarray(cum_op, _BF16), jnp.asarray(masks)


def _gla_level_operand(q, k, b, s, row8):
    half = s // 2
    n = b.shape[0]
    parts = []
    if half >= SUBLANES:
        for r0 in range(0, n, half):
            rows = slice(r0, r0 + half)
            r_bound = (r0 // s) * s + half - 1
            b_r = b[r_bound:r_bound + 1, :]
            if (r0 // half) % 2:
                parts.append(q[rows] * jnp.exp2(b[rows] - b_r))
            else:
                parts.append(k[rows] * jnp.exp2(b_r - b[rows]))
    else:
        low = (row8 & (s - 1)) >= half
        sign = jnp.where(low, -1.0, 1.0)
        for r0 in range(0, n, SUBLANES):
            rows = slice(r0, r0 + SUBLANES)
            b8 = b[rows]
            if half == 1:
                b_r = jnp.where(low, pltpu.roll(b8, 1, axis=0), b8)
            else:
                b_r = b8[half - 1:half, :]
                for blk in range(1, SUBLANES // s):
                    b_r = jnp.where(row8 >= blk * s, b8[blk * s + half - 1:blk * s + half, :], b_r)
            parts.append(jnp.where(low, q[rows], k[rows]) * jnp.exp2((b_r - b8) * sign))
    return jnp.concatenate(parts, axis=0).astype(_BF16)


def _gla_prepare_head(q, k, v, la, cum_op, mask_ref, row8):
    L, G = GLA_CHUNK, GLA_GROUP
    nlev = L.bit_length() - 1
    dk = q.shape[1]
    b = _bf16_split_dot(cum_op, la)
    b_last = [b[(c + 1) * L - 1:(c + 1) * L, :] for c in range(G)]
    q_in = (q * jnp.exp2(b)).astype(_BF16)
    k_dec = jnp.concatenate(
        [k[c * L:(c + 1) * L] * jnp.exp2(b_last[c] - b[c * L:(c + 1) * L]) for c in range(G)], axis=0
    ).astype(_BF16)
    decays = [jnp.exp2(bl) for bl in b_last]

    acc = [None] * G

    def add_level(idx, p):
        for c in range(G):
            t = c // 2
            blk = p[c * L:(c + 1) * L, t * LANES:(t + 1) * LANES] * mask_ref[idx, c % 2]
            acc[c] = blk if acc[c] is None else acc[c] + blk

    add_level(0, lax.dot_general(q.astype(_BF16), k.astype(_BF16), _NT, preferred_element_type=_F32))
    for lev in range(nlev):
        x = _gla_level_operand(q, k, b, L >> lev, row8)
        add_level(lev + 1, lax.dot_general(x, x, _NT, preferred_element_type=_F32))

    zeros = jnp.zeros((L, LANES), _F32)
    a_bd = jnp.concatenate(
        [jnp.concatenate([acc[c] if t == c // 2 else zeros for t in range(G // 2)], axis=1) for c in range(G)],
        axis=0).astype(_BF16)
    o_intra = jnp.dot(a_bd, v, preferred_element_type=_F32)
    return o_intra, q_in, k_dec, decays


def _gla_body(q_ref, k_ref, v_ref, g_ref, a_ref, wg_ref, bg_ref, gn_ref, cum_ref, mask_ref, o_ref,
              state_ref, oin_ref, qin_ref, kdec_ref, dec_ref, *, dk, dv):
    L, G = GLA_CHUNK, GLA_GROUP

    @pl.when(pl.program_id(1) == 0)
    def _():
        state_ref[...] = jnp.zeros_like(state_ref)

    z = jnp.dot(a_ref[...].astype(_BF16), wg_ref[...], preferred_element_type=_F32) + bg_ref[...]
    la_all = (jnp.minimum(z, 0.0) - jnp.log(1.0 + jnp.exp(-jnp.abs(z)))) * (LOG2E / GATE_TAU)

    row8 = lax.broadcasted_iota(jnp.int32, (SUBLANES, dk), 0)
    scale = dk ** -0.5
    cum_op = cum_ref[...]
    for hd in range(GLA_HEADS_PER_STEP):
        kc = slice(hd * dk, (hd + 1) * dk)
        vc = slice(hd * dv, (hd + 1) * dv)
        o_intra, q_in, k_dec, decays = _gla_prepare_head(
            q_ref[:, kc].astype(_F32) * scale, k_ref[:, kc].astype(_F32), v_ref[:, vc], la_all[:, kc],
            cum_op, mask_ref, row8)
        oin_ref[:, vc] = o_intra
        qin_ref[:, kc] = q_in
        kdec_ref[:, kc] = k_dec
        for c in range(G):
            dec_ref[c:c + 1, kc] = decays[c]

    for c in range(G):
        rows = slice(c * L, (c + 1) * L)
        for hd in range(GLA_HEADS_PER_STEP):
            kc = slice(hd * dk, (hd + 1) * dk)
            vc = slice(hd * dv, (hd + 1) * dv)
            st = state_ref[hd]
            o = oin_ref[rows, vc] + lax.dot_general(qin_ref[rows, kc], st.astype(_BF16), _NT,
                                                    preferred_element_type=_F32)
            upd = lax.dot_general(v_ref[rows, vc], kdec_ref[rows, kc], _TN, preferred_element_type=_F32)
            state_ref[hd] = st * dec_ref[c:c + 1, kc] + upd
            o = o * lax.rsqrt(jnp.mean(o * o, axis=-1, keepdims=True) + EPS)
            g = g_ref[rows, vc].astype(_F32)
            o_ref[rows, vc] = (o * gn_ref[:, vc] * (g * jax.nn.sigmoid(g))).astype(o_ref.dtype)


def _gla(proj, a_lr, w_gate_pad, b_gate, gla_norm, *, dk, dv):
    s = proj.shape[0]
    hp = GLA_HEADS_PER_STEP
    tb = GLA_CHUNK * GLA_GROUP
    kw, vw = hp * dk, hp * dv
    q_blk = 0
    k_blk = q_blk + GLA_HEADS // hp
    v_blk = (2 * GLA_HEADS * dk) // vw
    g_blk = v_blk + GLA_HEADS // hp
    assert v_blk * vw == 2 * GLA_HEADS * dk and proj.shape[1] == 2 * GLA_HEADS * (dk + dv)
    assert 2 * GLA_CHUNK == LANES and GLA_GROUP % 2 == 0 and s % tb == 0 and GLA_HEADS % hp == 0
    na = a_lr.shape[1]
    cum_op, pair_mask = _gla_constants(GLA_CHUNK, GLA_GROUP)
    body = functools.partial(_gla_body, dk=dk, dv=dv)
    return pl.pallas_call(
        body,
        out_shape=jax.ShapeDtypeStruct((s, GLA_HEADS * dv), _BF16),
        grid=(GLA_HEADS // hp, s // tb),
        in_specs=[pl.BlockSpec((tb, kw), lambda hg, t: (t, q_blk + hg)),
                  pl.BlockSpec((tb, kw), lambda hg, t: (t, k_blk + hg)),
                  pl.BlockSpec((tb, vw), lambda hg, t: (t, v_blk + hg)),
                  pl.BlockSpec((tb, vw), lambda hg, t: (t, g_blk + hg)),
                  pl.BlockSpec((tb, na), lambda hg, t: (t, 0)),
                  pl.BlockSpec((na, kw), lambda hg, t: (0, hg)),
                  pl.BlockSpec((1, kw), lambda hg, t: (0, hg)),
                  pl.BlockSpec((1, vw), lambda hg, t: (0, hg)),
                  pl.BlockSpec(cum_op.shape, lambda hg, t: (0, 0)),
                  pl.BlockSpec(pair_mask.shape, lambda hg, t: (0, 0, 0, 0))],
        out_specs=pl.BlockSpec((tb, vw), lambda hg, t: (t, hg)),
        scratch_shapes=[pltpu.VMEM((hp, dv, dk), _F32),
                        pltpu.VMEM((tb, vw), _F32),
                        pltpu.VMEM((tb, kw), _BF16),
                        pltpu.VMEM((tb, kw), _BF16),
                        pltpu.VMEM((SUBLANES, kw), _F32)],
        compiler_params=_params(("parallel", "arbitrary")),
        name="gla",
    )(proj, proj, proj, proj, a_lr, w_gate_pad, b_gate.reshape(1, -1).astype(_F32),
      gla_norm.reshape(1, -1).astype(_F32), cum_op, pair_mask)


def _xattn_body(res_ref, gin_ref, wq_ref, k_ref, v_ref, wo_ref, gain_ref, o_ref, hgo_ref, ssqo_ref, *, dh):
    scale = dh ** -0.5
    x = res_ref[...]
    r = lax.rsqrt(jnp.mean(x * x, axis=-1, keepdims=True) + EPS)
    q = jnp.dot((x * gin_ref[...]).astype(_BF16), wq_ref[...], preferred_element_type=_F32) * r
    q = q.astype(_BF16)
    heads = []
    for hd in range(XATTN_HEADS):
        sl = slice(hd * dh, (hd + 1) * dh)
        sc = lax.dot_general(q[:, sl], k_ref[:, sl], _NT, preferred_element_type=_F32) * scale
        p = jnp.exp(sc - jnp.max(sc, axis=-1, keepdims=True))
        denom = jnp.sum(p, axis=-1, keepdims=True)
        pv = jnp.dot(p.astype(_BF16), v_ref[:, sl], preferred_element_type=_F32)
        heads.append((pv / denom).astype(_BF16))
    att = jnp.concatenate(heads, axis=1)
    h = jnp.dot(att, wo_ref[...], preferred_element_type=_F32) + x
    o_ref[...] = h
    hgo_ref[...] = (h * gain_ref[...]).astype(hgo_ref.dtype)
    ssqo_ref[...] = _lane_partial_ssq(h)


def _xattn(res, gain, wq, wo, k, v, next_gain, *, bm=256):
    s, d = res.shape
    mlen, width = k.shape
    body = functools.partial(_xattn_body, dh=width // XATTN_HEADS)
    row_block = lambda w: pl.BlockSpec((bm, w), lambda i: (i, 0))
    whole = lambda a: pl.BlockSpec(a.shape, lambda i: (0, 0))
    w_block = lambda a: pl.BlockSpec(a.shape, lambda i: (0, 0), pipeline_mode=pl.Buffered(1))
    gain_row = pl.BlockSpec((1, d), lambda i: (0, 0))
    block_bytes = (2 * bm * 4 * d + 2 * bm * (4 * d + 2 * d + 4 * LANES)
                   + 2 * 2 * d * width + 2 * 2 * 2 * mlen * width)
    return pl.pallas_call(
        body,
        out_shape=(jax.ShapeDtypeStruct((s, d), _F32), jax.ShapeDtypeStruct((s, d), _BF16),
                   jax.ShapeDtypeStruct((s, LANES), _F32)),
        grid=(s // bm,),
        in_specs=[row_block(d), gain_row, w_block(wq), whole(k), whole(v), w_block(wo), gain_row],
        out_specs=(row_block(d), row_block(d), row_block(LANES)),
        compiler_params=_params(("parallel",), block_bytes),
        name="xattn",
    )(res, gain.reshape(1, d).astype(_F32), wq, k, v, wo, next_gain.reshape(1, d).astype(_F32))


def kernel(x, mem, norm_mix, w_in, w_conv, w_gate_up, b_gate, gla_norm, w_out, norm_xattn, norm_mem,
           wq_x, wk_x, wv_x, wo_x, norm_mlp, w_up, w_down, norm_final):
    bsz, seq, d = x.shape
    depth = w_in.shape[0]
    conv_width = w_conv.shape[2]
    gla_key = w_gate_up.shape[2]
    gla_width = gla_norm.shape[1]
    gate_rank = w_gate_up.shape[1]
    dk = gla_key // GLA_HEADS
    dv = gla_width // GLA_HEADS
    main_cols = 3 * conv_width + 2 * gla_key + 2 * gla_width
    assert bsz == 1 and w_in.shape[2] == main_cols + gate_rank

    w_gate_all = jnp.zeros((depth, LANES, gla_key), _BF16).at[:, :gate_rank].set(w_gate_up.astype(_BF16))
    assert conv_width % CONV_SEG == 0 and (main_cols - 3 * conv_width) % (3 * CONV_SEG) == 0
    conv_groups = conv_width // CONV_SEG
    w_in_t = jnp.swapaxes(w_in, 1, 2)
    w_in_l = _cast_transposed(w_in_t, 0, conv_groups)

    h = x.reshape(seq, d)
    mem2 = mem.reshape(mem.shape[1], d)
    for l in range(depth):
        w_a = jnp.zeros((d, LANES), _BF16).at[:, :gate_rank].set(w_in_l[:, main_cols:])
        w_taps = jnp.zeros((SUBLANES, conv_width), _F32).at[:CONV_K].set(w_conv[l].astype(_F32))
        hg, ssq, a_lr = _norm_prep(h, norm_mix[l], w_a)
        bn_in = 3 * CONV_SEG
        y_conv, w_up_l, w_out_l = _matmul(
            [hg], [(w_in_l, None, 0)], out_dtype=_BF16, row_ssq=ssq, conv=w_taps, bn=bn_in, cols=(0, conv_groups),
            side_casts=[(w_up, l, None), (w_out, l, None)], name="in_proj_conv")
        proj, wq_l, wo_l, *w_in_next = _matmul(
            [hg], [(w_in_l, None, 0)], out_dtype=_BF16, row_ssq=ssq, bn=bn_in,
            cols=(conv_groups, (main_cols - 3 * conv_width) // bn_in), name="in_proj_gla",
            side_casts=[(wq_x, l, None), (wo_x, l, None)] + ([(w_in_t, l + 1, conv_groups)] if l + 1 < depth else []))
        o_gla = _gla(proj, a_lr, w_gate_all[l], b_gate[l], gla_norm[l], dk=dk, dv=dv)
        h = _matmul([y_conv, o_gla], [(w_out_l, None, 0), (w_out_l, None, conv_width)], out_dtype=_F32, res=h,
                    bk=conv_width, name="out_proj")
        memn = _rmsnorm(mem2, norm_mem[l], _BF16, 256)
        k = _matmul([memn], [(wk_x, l, 0)], out_dtype=_BF16, bn=512, name="xattn_k")
        v = _matmul([memn], [(wv_x, l, 0)], out_dtype=_BF16, bn=512, name="xattn_v")
        h, hg, ssq = _xattn(h, norm_xattn[l], wq_l, wo_l, k, v, norm_mlp[l])
        hid, w_down_l = _matmul([hg], [(w_up_l, None, 0)], out_dtype=_BF16, act="relu2", row_ssq=ssq,
                                side_casts=[(w_down, l, None)], name="mlp_up")
        h = _matmul([hid], [(w_down_l, None, 0)], out_dtype=_F32, res=h, name="mlp_down")
        w_in_l = w_in_next[0] if w_in_next else None
    out = _rmsnorm(h, norm_final, _F32, 512)
    return out.reshape(bsz, seq, d)
```

```python
import functools
import math

import jax
import jax.numpy as jnp
import numpy as np
from jax import lax
from jax.experimental import pallas as pl
from jax.experimental.pallas import tpu as pltpu

EPS = 1e-6
GLA_CHUNK = 64
GLA_GROUP = 4
GLA_HEADS = 4
GLA_HEADS_PER_STEP = 4
XATTN_HEADS = 4
GATE_TAU = 16.0
CONV_K = 3
MXU_COLS = 256
CONV_SEG = MXU_COLS
LANES = 128
SUBLANES = 8
MIB = 1024 * 1024
VMEM_CAP = 60 * MIB
VMEM_TEMPS = 12 * MIB
VMEM_DEFAULT = 40 * MIB
LOG2E = math.log2(math.e)

_BF16 = jnp.bfloat16
_F32 = jnp.float32
_NT = (((1,), (1,)), ((), ()))
_TN = (((0,), (0,)), ((), ()))


def _params(sem, block_bytes=None):
    limit = VMEM_DEFAULT if block_bytes is None else min(VMEM_CAP, block_bytes + VMEM_TEMPS)
    return pltpu.CompilerParams(dimension_semantics=sem, vmem_limit_bytes=limit)


def _lane_partial_ssq(h):
    sq = h * h
    acc = sq[:, :LANES]
    for t in range(1, h.shape[1] // LANES):
        acc = acc + sq[:, t * LANES:(t + 1) * LANES]
    return acc


def _row_rsqrt(ssq_ref, inv_d):
    return lax.rsqrt(jnp.sum(ssq_ref[...], axis=-1, keepdims=True) * inv_d + EPS)


def _rmsnorm_body(x_ref, g_ref, o_ref):
    x = x_ref[...]
    y = x * lax.rsqrt(jnp.mean(x * x, axis=-1, keepdims=True) + EPS)
    o_ref[...] = (y * g_ref[...]).astype(o_ref.dtype)


def _rmsnorm(x, gain, out_dtype, bm):
    m, d = x.shape
    bm = min(bm, m)
    return pl.pallas_call(
        _rmsnorm_body,
        out_shape=jax.ShapeDtypeStruct((m, d), out_dtype),
        grid=(m // bm,),
        in_specs=[pl.BlockSpec((bm, d), lambda i: (i, 0)), pl.BlockSpec((1, d), lambda i: (0, 0))],
        out_specs=pl.BlockSpec((bm, d), lambda i: (i, 0)),
        compiler_params=_params(("parallel",)),
        name="rmsnorm",
    )(x, gain.reshape(1, d).astype(_F32))


def _norm_prep_body(x_ref, g_ref, wa_ref, hg_ref, ssq_ref, a_ref, *, inv_d):
    x = x_ref[...]
    hg = (x * g_ref[...]).astype(hg_ref.dtype)
    hg_ref[...] = hg
    ssq = _lane_partial_ssq(x)
    ssq_ref[...] = ssq
    r = lax.rsqrt(jnp.sum(ssq, axis=-1, keepdims=True) * inv_d + EPS)
    a_ref[...] = jnp.dot(hg, wa_ref[...], preferred_element_type=_F32) * r


def _norm_prep(x, gain, w_a, bm=512):
    m, d = x.shape
    na = w_a.shape[1]
    row_block = lambda width: pl.BlockSpec((bm, width), lambda i: (i, 0))
    return pl.pallas_call(
        functools.partial(_norm_prep_body, inv_d=1.0 / d),
        out_shape=(jax.ShapeDtypeStruct((m, d), _BF16), jax.ShapeDtypeStruct((m, LANES), _F32),
                   jax.ShapeDtypeStruct((m, na), _F32)),
        grid=(m // bm,),
        in_specs=[row_block(d), pl.BlockSpec((1, d), lambda i: (0, 0)),
                  pl.BlockSpec((d, na), lambda i: (0, 0))],
        out_specs=(row_block(d), row_block(LANES), row_block(na)),
        compiler_params=_params(("parallel",)),
        name="norm_prep",
    )(x, gain.reshape(1, d).astype(_F32), w_a)


def _mm_body(*refs, npairs, grid, act, has_res, inv_d, cast_w, sides, has_conv):
    it = iter(refs)
    xs = [next(it) for _ in range(npairs)]
    ws = [next(it) for _ in range(npairs)]
    res_ref = next(it) if has_res else None
    ssq_in_ref = next(it) if inv_d is not None else None
    wconv_ref = next(it) if has_conv else None
    side_src = [next(it) for _ in sides]
    o_ref = next(it)
    side_dst = [next(it) for _ in sides]
    tail_ref = next(it) if has_conv else None
    nk = grid[2]

    step = (pl.program_id(0) * grid[1] + pl.program_id(1)) * nk + pl.program_id(2)
    for (transposed, nblocks), src, dst in zip(sides, side_src, side_dst):
        def cast(src=src, dst=dst, transposed=transposed):
            val = src[...]
            dst[...] = (val.T if transposed else val).astype(dst.dtype)

        if nblocks == grid[0] * grid[1] * nk:
            cast()
        else:
            pl.when(step < nblocks)(cast)

    bn = ws[0].shape[1]
    chunk = min(MXU_COLS, bn)

    def product(c0):
        cs = slice(c0, c0 + chunk)
        acc = None
        for x_ref, w_ref in zip(xs, ws):
            w = w_ref[:, cs]
            part = jnp.dot(x_ref[...], w.astype(_BF16) if cast_w else w, preferred_element_type=_F32)
            acc = part if acc is None else acc + part
        return acc

    if nk > 1:
        @pl.when(pl.program_id(2) == 0)
        def _():
            o_ref[...] = res_ref[...] if has_res else jnp.zeros_like(o_ref)

        for c0 in range(0, bn, chunk):
            o_ref[:, c0:c0 + chunk] += product(c0)
        return

    row_scale = _row_rsqrt(ssq_in_ref, inv_d) if inv_d is not None else None

    def finished(c0):
        acc = product(c0)
        if row_scale is not None:
            acc = acc * row_scale
        if act == "relu2":
            r = jnp.maximum(acc, 0.0)
            acc = r * r
        if has_res:
            acc = acc + res_ref[:, c0:c0 + chunk]
        return acc

    if not has_conv:
        for c0 in range(0, bn, chunk):
            o_ref[:, c0:c0 + chunk] = finished(c0).astype(o_ref.dtype)
        return

    j = pl.program_id(1)

    @pl.when(step == 0)
    def _():
        tail_ref[...] = jnp.zeros_like(tail_ref)

    b_gate, c_gate, h_in = (finished(c0) for c0 in range(0, bn, chunk))
    u = c_gate * h_in
    rows = u.shape[0]
    row = lax.broadcasted_iota(jnp.int32, u.shape, 0)
    tail = tail_ref[j]
    prev1 = tail[SUBLANES - 1:SUBLANES, :]
    prev2 = tail[SUBLANES - 2:SUBLANES - 1, :]
    u1 = jnp.where(row == 0, prev1, pltpu.roll(u, 1, axis=0))
    u2 = jnp.where(row == 0, prev2, jnp.where(row == 1, prev1, pltpu.roll(u, 2, axis=0)))
    w = wconv_ref[...]
    y = b_gate * (w[0:1, :] * u2 + w[1:2, :] * u1 + w[2:3, :] * u)
    o_ref[...] = y.astype(o_ref.dtype)
    tail_ref[j] = u[rows - SUBLANES:, :]


def _conv_group_order(blk, nseg):
    return jnp.where(blk < 3 * nseg, 3 * (blk % nseg) + blk // nseg, blk)


def _matmul(xs, ws, *, out_dtype, res=None, act=None, row_ssq=None, side_casts=(), conv=None, cols=None,
            bm=1024, bn=1024, bk=4096, name="matmul"):
    cast_w = ws[0][0].dtype == _F32
    w_bytes = 4 if cast_w else 2
    npairs = len(xs)
    m, kdim = xs[0].shape
    n = ws[0][0].shape[-1]
    bm, bn, bk = min(bm, m), min(bn, n), min(bk, kdim)
    nk = kdim // bk
    assert m % bm == 0 and kdim % bk == 0
    assert nk == 1 or (act is None and row_ssq is None and out_dtype == _F32)
    j0, ntiles = cols if cols is not None else (0, n // bn)
    grid = (m // bm, ntiles, nk)
    nsteps = grid[0] * grid[1] * nk
    in_specs = [pl.BlockSpec((bm, bk), lambda i, j, k: (i, k))] * npairs
    for w, layer, row0 in ws:
        assert row0 % bk == 0
        kb = row0 // bk
        if layer is None:
            in_specs.append(pl.BlockSpec((bk, bn), lambda i, j, k, kb=kb: (kb + k, j0 + j)))
        else:
            in_specs.append(pl.BlockSpec((None, bk, bn), lambda i, j, k, kb=kb, layer=layer: (layer, kb + k, j0 + j)))
    args = list(xs) + [w for w, _, _ in ws]
    out_bytes = jnp.dtype(out_dtype).itemsize
    block_bytes = 2 * (npairs * (2 * bm * bk + w_bytes * bk * bn) + bm * bn * out_bytes)
    if res is not None:
        in_specs.append(pl.BlockSpec((bm, bn), lambda i, j, k: (i, j)))
        args.append(res)
        block_bytes += 2 * 4 * bm * bn
    inv_d = None
    if row_ssq is not None:
        in_specs.append(pl.BlockSpec((bm, LANES), lambda i, j, k: (i, 0)))
        args.append(row_ssq)
        inv_d = 1.0 / kdim
        block_bytes += 2 * 4 * bm * LANES
    scratch = []
    if conv is None:
        out_shape = [jax.ShapeDtypeStruct((m, ntiles * bn), out_dtype)]
        out_specs = [pl.BlockSpec((bm, bn), lambda i, j, k: (i, j))]
    else:
        seg = bn // 3
        assert seg == CONV_SEG and nk == 1 and conv.shape[1] == ntiles * seg
        in_specs.append(pl.BlockSpec((SUBLANES, seg), lambda i, j, k: (0, j)))
        args.append(conv)
        out_shape = [jax.ShapeDtypeStruct((m, ntiles * seg), out_dtype)]
        out_specs = [pl.BlockSpec((bm, seg), lambda i, j, k: (i, j))]
        scratch = [pltpu.VMEM((ntiles, SUBLANES, seg), _F32)]
    sides = []
    for src, layer, conv_groups in side_casts:
        _, rows, cols = src.shape
        transposed = conv_groups is not None
        if transposed:
            slab = CONV_SEG
            nblocks = pl.cdiv(rows, slab)
            out_shape.append(jax.ShapeDtypeStruct((cols, rows), _BF16))
        else:
            tile = 2 * SUBLANES
            slab = next(c for c in range(tile, rows + 1, tile) if rows % c == 0 and rows // c <= nsteps)
            nblocks = rows // slab
            out_shape.append(jax.ShapeDtypeStruct((rows, cols), _BF16))
        assert nblocks <= nsteps
        blk = lambda i, j, k, last=nblocks - 1: jnp.minimum((i * grid[1] + j) * nk + k, last)
        in_specs.append(pl.BlockSpec((None, slab, cols), lambda i, j, k, layer=layer, blk=blk: (layer, blk(i, j, k), 0)))
        args.append(src)
        if transposed:
            out_specs.append(pl.BlockSpec(
                (cols, slab), lambda i, j, k, blk=blk, g=conv_groups: (0, _conv_group_order(blk(i, j, k), g))))
        else:
            out_specs.append(pl.BlockSpec((slab, cols), lambda i, j, k, blk=blk: (blk(i, j, k), 0)))
        sides.append((transposed, nblocks))
        block_bytes += 2 * (4 + 2) * slab * cols
    body = functools.partial(_mm_body, npairs=npairs, grid=grid, act=act, has_res=res is not None, inv_d=inv_d,
                             cast_w=cast_w, sides=tuple(sides), has_conv=conv is not None)
    outs = pl.pallas_call(
        body,
        out_shape=tuple(out_shape),
        grid=grid,
        in_specs=in_specs,
        out_specs=tuple(out_specs),
        scratch_shapes=scratch,
        compiler_params=_params(("arbitrary", "arbitrary", "arbitrary"), block_bytes),
        name=name,
    )(*args)
    return outs[0] if len(outs) == 1 else outs


def _cast_transposed_body(src_ref, dst_ref):
    dst_ref[...] = src_ref[...].T.astype(dst_ref.dtype)


def _cast_transposed(w_t, layer, conv_groups):
    _, n, kdim = w_t.shape
    bn = CONV_SEG
    return pl.pallas_call(
        _cast_transposed_body,
        out_shape=jax.ShapeDtypeStruct((kdim, n), _BF16),
        grid=(pl.cdiv(n, bn),),
        in_specs=[pl.BlockSpec((None, bn, kdim), lambda t: (layer, t, 0))],
        out_specs=pl.BlockSpec((kdim, bn), lambda t: (0, _conv_group_order(t, conv_groups))),
        compiler_params=_params(("parallel",)),
        name="cast_transposed",
    )(w_t)


def _bf16_split_dot(a_exact_bf16, x):
    hi = x.astype(_BF16)
    lo = (x - hi.astype(_F32)).astype(_BF16)
    return (jnp.dot(a_exact_bf16, hi, preferred_element_type=_F32)
            + jnp.dot(a_exact_bf16, lo, preferred_element_type=_F32))


def _gla_constants(L, G):
    nlev = L.bit_length() - 1
    r = np.arange(L * G)
    cum_op = (r[None, :] <= r[:, None]) & (r[None, :] // L == r[:, None] // L)
    i = np.arange(L)
    base = [np.eye(L, dtype=bool)]
    for lev in range(nlev):
        s = L >> lev
        lower = (i % s) >= s // 2
        base.append((i[:, None] // s == i[None, :] // s) & lower[:, None] & ~lower[None, :])
    base = np.stack(base).astype(np.float32)
    masks = np.zeros((nlev + 1, 2, L, 2 * L), np.float32)
    masks[:, 0, :, :L] = base
    masks[:, 1, :, L:] = base
    return jnp.asarray(cum_op, _BF16), jnp.asarray(masks)


def _gla_level_operand(q, k, b, s, row8):
    half = s // 2
    n = b.shape[0]
    parts = []
    if half >= SUBLANES:
        for r0 in range(0, n, half):
            rows = slice(r0, r0 + half)
            r_bound = (r0 // s) * s + half - 1
            b_r = b[r_bound:r_bound + 1, :]
            if (r0 // half) % 2:
                parts.append(q[rows] * jnp.exp2(b[rows] - b_r))
            else:
                parts.append(k[rows] * jnp.exp2(b_r - b[rows]))
    else:
        low = (row8 & (s - 1)) >= half
        sign = jnp.where(low, -1.0, 1.0)
        for r0 in range(0, n, SUBLANES):
            rows = slice(r0, r0 + SUBLANES)
            b8 = b[rows]
            if half == 1:
                b_r = jnp.where(low, pltpu.roll(b8, 1, axis=0), b8)
            else:
                b_r = b8[half - 1:half, :]
                for blk in range(1, SUBLANES // s):
                    b_r = jnp.where(row8 >= blk * s, b8[blk * s + half - 1:blk * s + half, :], b_r)
            parts.append(jnp.where(low, q[rows], k[rows]) * jnp.exp2((b_r - b8) * sign))
    return jnp.concatenate(parts, axis=0).astype(_BF16)


def _gla_prepare_head(q, k, v, la, cum_op, mask_ref, row8):
    L, G = GLA_CHUNK, GLA_GROUP
    nlev = L.bit_length() - 1
    dk = q.shape[1]
    b = _bf16_split_dot(cum_op, la)
    b_last = [b[(c + 1) * L - 1:(c + 1) * L, :] for c in range(G)]
    q_in = (q * jnp.exp2(b)).astype(_BF16)
    k_dec = jnp.concatenate(
        [k[c * L:(c + 1) * L] * jnp.exp2(b_last[c] - b[c * L:(c + 1) * L]) for c in range(G)], axis=0
    ).astype(_BF16)
    decays = [jnp.exp2(bl) for bl in b_last]

    acc = [None] * G

    def add_level(idx, p):
        for c in range(G):
            t = c // 2
            blk = p[c * L:(c + 1) * L, t * LANES:(t + 1) * LANES] * mask_ref[idx, c % 2]
            acc[c] = blk if acc[c] is None else acc[c] + blk

    add_level(0, lax.dot_general(q.astype(_BF16), k.astype(_BF16), _NT, preferred_element_type=_F32))
    for lev in range(nlev):
        x = _gla_level_operand(q, k, b, L >> lev, row8)
        add_level(lev + 1, lax.dot_general(x, x, _NT, preferred_element_type=_F32))

    zeros = jnp.zeros((L, LANES), _F32)
    a_bd = jnp.concatenate(
        [jnp.concatenate([acc[c] if t == c // 2 else zeros for t in range(G // 2)], axis=1) for c in range(G)],
        axis=0).astype(_BF16)
    o_intra = jnp.dot(a_bd, v, preferred_element_type=_F32)
    return o_intra, q_in, k_dec, decays


def _gla_body(q_ref, k_ref, v_ref, g_ref, a_ref, wg_ref, bg_ref, gn_ref, cum_ref, mask_ref, o_ref,
              state_ref, oin_ref, qin_ref, kdec_ref, dec_ref, *, dk, dv):
    L, G = GLA_CHUNK, GLA_GROUP

    @pl.when(pl.program_id(1) == 0)
    def _():
        state_ref[...] = jnp.zeros_like(state_ref)

    z = jnp.dot(a_ref[...].astype(_BF16), wg_ref[...], preferred_element_type=_F32) + bg_ref[...]
    la_all = (jnp.minimum(z, 0.0) - jnp.log(1.0 + jnp.exp(-jnp.abs(z)))) * (LOG2E / GATE_TAU)

    row8 = lax.broadcasted_iota(jnp.int32, (SUBLANES, dk), 0)
    scale = dk ** -0.5
    cum_op = cum_ref[...]
    for hd in range(GLA_HEADS_PER_STEP):
        kc = slice(hd * dk, (hd + 1) * dk)
        vc = slice(hd * dv, (hd + 1) * dv)
        o_intra, q_in, k_dec, decays = _gla_prepare_head(
            q_ref[:, kc].astype(_F32) * scale, k_ref[:, kc].astype(_F32), v_ref[:, vc], la_all[:, kc],
            cum_op, mask_ref, row8)
        oin_ref[:, vc] = o_intra
        qin_ref[:, kc] = q_in
        kdec_ref[:, kc] = k_dec
        for c in range(G):
            dec_ref[c:c + 1, kc] = decays[c]

    for c in range(G):
        rows = slice(c * L, (c + 1) * L)
        for hd in range(GLA_HEADS_PER_STEP):
            kc = slice(hd * dk, (hd + 1) * dk)
            vc = slice(hd * dv, (hd + 1) * dv)
            st = state_ref[hd]
            o = oin_ref[rows, vc] + lax.dot_general(qin_ref[rows, kc], st.astype(_BF16), _NT,
                                                    preferred_element_type=_F32)
            upd = lax.dot_general(v_ref[rows, vc], kdec_ref[rows, kc], _TN, preferred_element_type=_F32)
            state_ref[hd] = st * dec_ref[c:c + 1, kc] + upd
            o = o * lax.rsqrt(jnp.mean(o * o, axis=-1, keepdims=True) + EPS)
            g = g_ref[rows, vc].astype(_F32)
            o_ref[rows, vc] = (o * gn_ref[:, vc] * (g * jax.nn.sigmoid(g))).astype(o_ref.dtype)


def _gla(proj, a_lr, w_gate_pad, b_gate, gla_norm, *, dk, dv):
    s = proj.shape[0]
    hp = GLA_HEADS_PER_STEP
    tb = GLA_CHUNK * GLA_GROUP
    kw, vw = hp * dk, hp * dv
    q_blk = 0
    k_blk = q_blk + GLA_HEADS // hp
    v_blk = (2 * GLA_HEADS * dk) // vw
    g_blk = v_blk + GLA_HEADS // hp
    assert v_blk * vw == 2 * GLA_HEADS * dk and proj.shape[1] == 2 * GLA_HEADS * (dk + dv)
    assert 2 * GLA_CHUNK == LANES and GLA_GROUP % 2 == 0 and s % tb == 0 and GLA_HEADS % hp == 0
    na = a_lr.shape[1]
    cum_op, pair_mask = _gla_constants(GLA_CHUNK, GLA_GROUP)
    body = functools.partial(_gla_body, dk=dk, dv=dv)
    return pl.pallas_call(
        body,
        out_shape=jax.ShapeDtypeStruct((s, GLA_HEADS * dv), _BF16),
        grid=(GLA_HEADS // hp, s // tb),
        in_specs=[pl.BlockSpec((tb, kw), lambda hg, t: (t, q_blk + hg)),
                  pl.BlockSpec((tb, kw), lambda hg, t: (t, k_blk + hg)),
                  pl.BlockSpec((tb, vw), lambda hg, t: (t, v_blk + hg)),
                  pl.BlockSpec((tb, vw), lambda hg, t: (t, g_blk + hg)),
                  pl.BlockSpec((tb, na), lambda hg, t: (t, 0)),
                  pl.BlockSpec((na, kw), lambda hg, t: (0, hg)),
                  pl.BlockSpec((1, kw), lambda hg, t: (0, hg)),
                  pl.BlockSpec((1, vw), lambda hg, t: (0, hg)),
                  pl.BlockSpec(cum_op.shape, lambda hg, t: (0, 0)),
                  pl.BlockSpec(pair_mask.shape, lambda hg, t: (0, 0, 0, 0))],
        out_specs=pl.BlockSpec((tb, vw), lambda hg, t: (t, hg)),
        scratch_shapes=[pltpu.VMEM((hp, dv, dk), _F32),
                        pltpu.VMEM((tb, vw), _F32),
                        pltpu.VMEM((tb, kw), _BF16),
                        pltpu.VMEM((tb, kw), _BF16),
                        pltpu.VMEM((SUBLANES, kw), _F32)],
        compiler_params=_params(("parallel", "arbitrary")),
        name="gla",
    )(proj, proj, proj, proj, a_lr, w_gate_pad, b_gate.reshape(1, -1).astype(_F32),
      gla_norm.reshape(1, -1).astype(_F32), cum_op, pair_mask)


def _xattn_body(res_ref, gin_ref, wq_ref, k_ref, v_ref, wo_ref, gain_ref, o_ref, hgo_ref, ssqo_ref, *, dh):
    scale = dh ** -0.5
    x = res_ref[...]
    r = lax.rsqrt(jnp.mean(x * x, axis=-1, keepdims=True) + EPS)
    q = jnp.dot((x * gin_ref[...]).astype(_BF16), wq_ref[...], preferred_element_type=_F32) * r
    q = q.astype(_BF16)
    heads = []
    for hd in range(XATTN_HEADS):
        sl = slice(hd * dh, (hd + 1) * dh)
        sc = lax.dot_general(q[:, sl], k_ref[:, sl], _NT, preferred_element_type=_F32) * scale
        p = jnp.exp(sc - jnp.max(sc, axis=-1, keepdims=True))
        denom = jnp.sum(p, axis=-1, keepdims=True)
        pv = jnp.dot(p.astype(_BF16), v_ref[:, sl], preferred_element_type=_F32)
        heads.append((pv / denom).astype(_BF16))
    att = jnp.concatenate(heads, axis=1)
    h = jnp.dot(att, wo_ref[...], preferred_element_type=_F32) + x
    o_ref[...] = h
    hgo_ref[...] = (h * gain_ref[...]).astype(hgo_ref.dtype)
    ssqo_ref[...] = _lane_partial_ssq(h)


def _xattn(res, gain, wq, wo, k, v, next_gain, *, bm=256):
    s, d = res.shape
    mlen, width = k.shape
    body = functools.partial(_xattn_body, dh=width // XATTN_HEADS)
    row_block = lambda w: pl.BlockSpec((bm, w), lambda i: (i, 0))
    whole = lambda a: pl.BlockSpec(a.shape, lambda i: (0, 0))
    w_block = lambda a: pl.BlockSpec(a.shape, lambda i: (0, 0), pipeline_mode=pl.Buffered(1))
    gain_row = pl.BlockSpec((1, d), lambda i: (0, 0))
    block_bytes = (2 * bm * 4 * d + 2 * bm * (4 * d + 2 * d + 4 * LANES)
                   + 2 * 2 * d * width + 2 * 2 * 2 * mlen * width)
    return pl.pallas_call(
        body,
        out_shape=(jax.ShapeDtypeStruct((s, d), _F32), jax.ShapeDtypeStruct((s, d), _BF16),
                   jax.ShapeDtypeStruct((s, LANES), _F32)),
        grid=(s // bm,),
        in_specs=[row_block(d), gain_row, w_block(wq), whole(k), whole(v), w_block(wo), gain_row],
        out_specs=(row_block(d), row_block(d), row_block(LANES)),
        compiler_params=_params(("parallel",), block_bytes),
        name="xattn",
    )(res, gain.reshape(1, d).astype(_F32), wq, k, v, wo, next_gain.reshape(1, d).astype(_F32))


def kernel(x, mem, norm_mix, w_in, w_conv, w_gate_up, b_gate, gla_norm, w_out, norm_xattn, norm_mem,
           wq_x, wk_x, wv_x, wo_x, norm_mlp, w_up, w_down, norm_final):
    bsz, seq, d = x.shape
    depth = w_in.shape[0]
    conv_width = w_conv.shape[2]
    gla_key = w_gate_up.shape[2]
    gla_width = gla_norm.shape[1]
    gate_rank = w_gate_up.shape[1]
    dk = gla_key // GLA_HEADS
    dv = gla_width // GLA_HEADS
    main_cols = 3 * conv_width + 2 * gla_key + 2 * gla_width
    assert bsz == 1 and w_in.shape[2] == main_cols + gate_rank

    w_gate_all = jnp.zeros((depth, LANES, gla_key), _BF16).at[:, :gate_rank].set(w_gate_up.astype(_BF16))
    assert conv_width % CONV_SEG == 0 and (main_cols - 3 * conv_width) % (3 * CONV_SEG) == 0
    conv_groups = conv_width // CONV_SEG
    w_in_t = jnp.swapaxes(w_in, 1, 2)
    w_in_l = _cast_transposed(w_in_t, 0, conv_groups)

    h = x.reshape(seq, d)
    mem2 = mem.reshape(mem.shape[1], d)
    for l in range(depth):
        w_a = jnp.zeros((d, LANES), _BF16).at[:, :gate_rank].set(w_in_l[:, main_cols:])
        w_taps = jnp.zeros((SUBLANES, conv_width), _F32).at[:CONV_K].set(w_conv[l].astype(_F32))
        hg, ssq, a_lr = _norm_prep(h, norm_mix[l], w_a)
        bn_in = 3 * CONV_SEG
        y_conv, w_up_l, w_out_l = _matmul(
            [hg], [(w_in_l, None, 0)], out_dtype=_BF16, row_ssq=ssq, conv=w_taps, bn=bn_in, cols=(0, conv_groups),
            side_casts=[(w_up, l, None), (w_out, l, None)], name="in_proj_conv")
        proj, wq_l, wo_l, *w_in_next = _matmul(
            [hg], [(w_in_l, None, 0)], out_dtype=_BF16, row_ssq=ssq, bn=bn_in,
            cols=(conv_groups, (main_cols - 3 * conv_width) // bn_in), name="in_proj_gla",
            side_casts=[(wq_x, l, None), (wo_x, l, None)] + ([(w_in_t, l + 1, conv_groups)] if l + 1 < depth else []))
        o_gla = _gla(proj, a_lr, w_gate_all[l], b_gate[l], gla_norm[l], dk=dk, dv=dv)
        h = _matmul([y_conv, o_gla], [(w_out_l, None, 0), (w_out_l, None, conv_width)], out_dtype=_F32, res=h,
                    bk=conv_width, name="out_proj")
        memn = _rmsnorm(mem2, norm_mem[l], _BF16, 256)
        k = _matmul([memn], [(wk_x, l, 0)], out_dtype=_BF16, bn=512, name="xattn_k")
        v = _matmul([memn], [(wv_x, l, 0)], out_dtype=_BF16, bn=512, name="xattn_v")
        h, hg, ssq = _xattn(h, norm_xattn[l], wq_l, wo_l, k, v, norm_mlp[l])
        hid, w_down_l = _matmul([hg], [(w_up_l, None, 0)], out_dtype=_BF16, act="relu2", row_ssq=ssq,
                                side_casts=[(w_down, l, None)], name="mlp_up")
        h = _matmul([hid], [(w_down_l, None, 0)], out_dtype=_F32, res=h, name="mlp_down")
        w_in_l = w_in_next[0] if w_in_next else None
    out = _rmsnorm(h, norm_final, _F32, 512)
    return out.reshape(bsz, seq, d)
```

```python
import functools
import math

import jax
import jax.numpy as jnp
import numpy as np
from jax import lax
from jax.experimental import pallas as pl
from jax.experimental.pallas import tpu as pltpu

EPS = 1e-6
GLA_CHUNK = 64
GLA_GROUP = 4
GLA_HEADS = 4
GLA_HEADS_PER_STEP = 4
XATTN_HEADS = 4
GATE_TAU = 16.0
CONV_K = 3
MXU_COLS = 256
CONV_SEG = MXU_COLS
LANES = 128
SUBLANES = 8
MIB = 1024 * 1024
VMEM_CAP = 60 * MIB
VMEM_TEMPS = 12 * MIB
VMEM_DEFAULT = 40 * MIB
LOG2E = math.log2(math.e)

_BF16 = jnp.bfloat16
_F32 = jnp.float32
_NT = (((1,), (1,)), ((), ()))
_TN = (((0,), (0,)), ((), ()))


def _params(sem, block_bytes=None):
    limit = VMEM_DEFAULT if block_bytes is None else min(VMEM_CAP, block_bytes + VMEM_TEMPS)
    return pltpu.CompilerParams(dimension_semantics=sem, vmem_limit_bytes=limit)


def _lane_partial_ssq(h):
    sq = h * h
    acc = sq[:, :LANES]
    for t in range(1, h.shape[1] // LANES):
        acc = acc + sq[:, t * LANES:(t + 1) * LANES]
    return acc


def _row_rsqrt(ssq_ref, inv_d):
    return lax.rsqrt(jnp.sum(ssq_ref[...], axis=-1, keepdims=True) * inv_d + EPS)


def _rmsnorm_body(x_ref, g_ref, o_ref):
    x = x_ref[...]
    y = x * lax.rsqrt(jnp.mean(x * x, axis=-1, keepdims=True) + EPS)
    o_ref[...] = (y * g_ref[...]).astype(o_ref.dtype)


def _rmsnorm(x, gain, out_dtype, bm):
    m, d = x.shape
    bm = min(bm, m)
    return pl.pallas_call(
        _rmsnorm_body,
        out_shape=jax.ShapeDtypeStruct((m, d), out_dtype),
        grid=(m // bm,),
        in_specs=[pl.BlockSpec((bm, d), lambda i: (i, 0)), pl.BlockSpec((1, d), lambda i: (0, 0))],
        out_specs=pl.BlockSpec((bm, d), lambda i: (i, 0)),
        compiler_params=_params(("parallel",)),
        name="rmsnorm",
    )(x, gain.reshape(1, d).astype(_F32))


def _norm_prep_body(x_ref, g_ref, hg_ref, ssq_ref):
    x = x_ref[...]
    hg_ref[...] = (x * g_ref[...]).astype(hg_ref.dtype)
    ssq_ref[...] = _lane_partial_ssq(x)


def _norm_prep(x, gain, bm=512):
    m, d = x.shape
    row_block = lambda width: pl.BlockSpec((bm, width), lambda i: (i, 0))
    return pl.pallas_call(
        _norm_prep_body,
        out_shape=(jax.ShapeDtypeStruct((m, d), _BF16), jax.ShapeDtypeStruct((m, LANES), _F32)),
        grid=(m // bm,),
        in_specs=[row_block(d), pl.BlockSpec((1, d), lambda i: (0, 0))],
        out_specs=(row_block(d), row_block(LANES)),
        compiler_params=_params(("parallel",)),
        name="norm_prep",
    )(x, gain.reshape(1, d).astype(_F32))


def _mm_body(*refs, npairs, grid, act, has_res, inv_d, cast_w, sides, has_conv, emit_norm):
    it = iter(refs)
    xs = [next(it) for _ in range(npairs)]
    ws = [next(it) for _ in range(npairs)]
    res_ref = next(it) if has_res else None
    ssq_in_ref = next(it) if inv_d is not None else None
    wconv_ref, wgate_ref = (next(it), next(it)) if has_conv else (None, None)
    gain_ref = next(it) if emit_norm else None
    side_src = [next(it) for _ in sides]
    o_ref = next(it)
    gate_ref = next(it) if has_conv else None
    hg_ref, ssq_ref = (next(it), next(it)) if emit_norm else (None, None)
    side_dst = [next(it) for _ in sides]
    tail_ref = next(it) if has_conv else None
    nk = grid[2]

    step = (pl.program_id(0) * grid[1] + pl.program_id(1)) * nk + pl.program_id(2)
    for (transposed, nblocks), src, dst in zip(sides, side_src, side_dst):
        def cast(src=src, dst=dst, transposed=transposed):
            val = src[...]
            dst[...] = (val.T if transposed else val).astype(dst.dtype)

        if nblocks == grid[0] * grid[1] * nk:
            cast()
        else:
            pl.when(step < nblocks)(cast)

    bn = ws[0].shape[1]
    chunk = min(MXU_COLS, bn)

    def product(c0):
        cs = slice(c0, c0 + chunk)
        acc = None
        for x_ref, w_ref in zip(xs, ws):
            w = w_ref[:, cs]
            part = jnp.dot(x_ref[...], w.astype(_BF16) if cast_w else w, preferred_element_type=_F32)
            acc = part if acc is None else acc + part
        return acc

    if nk > 1:
        @pl.when(pl.program_id(2) == 0)
        def _():
            o_ref[...] = res_ref[...] if has_res else jnp.zeros_like(o_ref)

        def accumulate():
            for c0 in range(0, bn, chunk):
                o_ref[:, c0:c0 + chunk] += product(c0)

        if not emit_norm:
            accumulate()
            return

        def finish():
            partial = None
            for c0 in range(0, bn, chunk):
                cs = slice(c0, c0 + chunk)
                h = o_ref[:, cs] + product(c0)
                o_ref[:, cs] = h
                hg_ref[:, cs] = (h * gain_ref[:, cs]).astype(hg_ref.dtype)
                part = _lane_partial_ssq(h)
                partial = part if partial is None else partial + part
            first = pl.program_id(1) == 0

            @pl.when(first)
            def _():
                ssq_ref[...] = partial

            @pl.when(jnp.logical_not(first))
            def _():
                ssq_ref[...] += partial

        last = pl.program_id(2) == nk - 1
        pl.when(jnp.logical_not(last))(accumulate)
        pl.when(last)(finish)
        return

    row_scale = _row_rsqrt(ssq_in_ref, inv_d) if inv_d is not None else None

    def finished(c0):
        acc = product(c0)
        if row_scale is not None:
            acc = acc * row_scale
        if act == "relu2":
            r = jnp.maximum(acc, 0.0)
            acc = r * r
        if has_res:
            acc = acc + res_ref[:, c0:c0 + chunk]
        return acc

    if not has_conv:
        for c0 in range(0, bn, chunk):
            o_ref[:, c0:c0 + chunk] = finished(c0).astype(o_ref.dtype)
        return

    j = pl.program_id(1)

    @pl.when(step == 0)
    def _():
        tail_ref[...] = jnp.zeros_like(tail_ref)

    @pl.when(j == 0)
    def _():
        gate_ref[...] = jnp.dot(xs[0][...], wgate_ref[...], preferred_element_type=_F32) * row_scale

    b_gate, c_gate, h_in = (finished(c0) for c0 in range(0, bn, chunk))
    u = c_gate * h_in
    rows = u.shape[0]
    row = lax.broadcasted_iota(jnp.int32, u.shape, 0)
    tail = tail_ref[j]
    prev1 = tail[SUBLANES - 1:SUBLANES, :]
    prev2 = tail[SUBLANES - 2:SUBLANES - 1, :]
    u1 = jnp.where(row == 0, prev1, pltpu.roll(u, 1, axis=0))
    u2 = jnp.where(row == 0, prev2, jnp.where(row == 1, prev1, pltpu.roll(u, 2, axis=0)))
    w = wconv_ref[...]
    y = b_gate * (w[0:1, :] * u2 + w[1:2, :] * u1 + w[2:3, :] * u)
    o_ref[...] = y.astype(o_ref.dtype)
    tail_ref[j] = u[rows - SUBLANES:, :]


def _conv_group_order(blk, nseg):
    return jnp.where(blk < 3 * nseg, 3 * (blk % nseg) + blk // nseg, blk)


def _matmul(xs, ws, *, out_dtype, res=None, act=None, row_ssq=None, next_gain=None, side_casts=(), conv=None,
            cols=None, bm=1024, bn=1024, bk=4096, name="matmul"):
    cast_w = ws[0][0].dtype == _F32
    w_bytes = 4 if cast_w else 2
    npairs = len(xs)
    m, kdim = xs[0].shape
    n = ws[0][0].shape[-1]
    bm, bn, bk = min(bm, m), min(bn, n), min(bk, kdim)
    nk = kdim // bk
    assert m % bm == 0 and kdim % bk == 0
    assert nk == 1 or (act is None and row_ssq is None and out_dtype == _F32)
    assert next_gain is None or nk > 1
    j0, ntiles = cols if cols is not None else (0, n // bn)
    grid = (m // bm, ntiles, nk)
    nsteps = grid[0] * grid[1] * nk
    in_specs = [pl.BlockSpec((bm, bk), lambda i, j, k: (i, k))] * npairs
    for w, layer, row0 in ws:
        assert row0 % bk == 0
        kb = row0 // bk
        if layer is None:
            in_specs.append(pl.BlockSpec((bk, bn), lambda i, j, k, kb=kb: (kb + k, j0 + j)))
        else:
            in_specs.append(pl.BlockSpec((None, bk, bn), lambda i, j, k, kb=kb, layer=layer: (layer, kb + k, j0 + j)))
    args = list(xs) + [w for w, _, _ in ws]
    out_bytes = jnp.dtype(out_dtype).itemsize
    block_bytes = 2 * (npairs * (2 * bm * bk + w_bytes * bk * bn) + bm * bn * out_bytes)
    if res is not None:
        in_specs.append(pl.BlockSpec((bm, bn), lambda i, j, k: (i, j)))
        args.append(res)
        block_bytes += 2 * 4 * bm * bn
    inv_d = None
    if row_ssq is not None:
        in_specs.append(pl.BlockSpec((bm, LANES), lambda i, j, k: (i, 0)))
        args.append(row_ssq)
        inv_d = 1.0 / kdim
        block_bytes += 2 * 4 * bm * LANES
    scratch = []
    if conv is None:
        out_shape = [jax.ShapeDtypeStruct((m, ntiles * bn), out_dtype)]
        out_specs = [pl.BlockSpec((bm, bn), lambda i, j, k: (i, j))]
    else:
        taps, w_gate = conv
        seg = bn // 3
        assert seg == CONV_SEG and nk == 1 and taps.shape[1] == ntiles * seg and row_ssq is not None
        in_specs += [pl.BlockSpec((SUBLANES, seg), lambda i, j, k: (0, j)),
                     pl.BlockSpec(w_gate.shape, lambda i, j, k: (0, 0))]
        args += [taps, w_gate]
        out_shape = [jax.ShapeDtypeStruct((m, ntiles * seg), out_dtype),
                     jax.ShapeDtypeStruct((m, w_gate.shape[1]), _F32)]
        out_specs = [pl.BlockSpec((bm, seg), lambda i, j, k: (i, j)),
                     pl.BlockSpec((bm, w_gate.shape[1]), lambda i, j, k: (i, 0))]
        scratch = [pltpu.VMEM((ntiles, SUBLANES, seg), _F32)]
        block_bytes += 2 * (2 * w_gate.size + 4 * bm * w_gate.shape[1])
    if next_gain is not None:
        assert n == ntiles * bn
        in_specs.append(pl.BlockSpec((1, bn), lambda i, j, k: (0, j)))
        args.append(next_gain.reshape(1, n).astype(_F32))
        out_shape += [jax.ShapeDtypeStruct((m, n), _BF16), jax.ShapeDtypeStruct((m, LANES), _F32)]
        out_specs += [pl.BlockSpec((bm, bn), lambda i, j, k: (i, j)),
                      pl.BlockSpec((bm, LANES), lambda i, j, k: (i, 0))]
        block_bytes += 2 * (2 * bm * bn + 4 * bm * LANES)
    sides = []
    for src, layer, conv_groups in side_casts:
        _, rows, cols = src.shape
        transposed = conv_groups is not None
        if transposed:
            slab = CONV_SEG
            nblocks = pl.cdiv(rows, slab)
            out_shape.append(jax.ShapeDtypeStruct((cols, rows), _BF16))
        else:
            tile = 2 * SUBLANES
            slab = next(c for c in range(tile, rows + 1, tile) if rows % c == 0 and rows // c <= nsteps)
            nblocks = rows // slab
            out_shape.append(jax.ShapeDtypeStruct((rows, cols), _BF16))
        assert nblocks <= nsteps
        blk = lambda i, j, k, last=nblocks - 1: jnp.minimum((i * grid[1] + j) * nk + k, last)
        in_specs.append(pl.BlockSpec((None, slab, cols), lambda i, j, k, layer=layer, blk=blk: (layer, blk(i, j, k), 0)))
        args.append(src)
        if transposed:
            out_specs.append(pl.BlockSpec(
                (cols, slab), lambda i, j, k, blk=blk, g=conv_groups: (0, _conv_group_order(blk(i, j, k), g))))
        else:
            out_specs.append(pl.BlockSpec((slab, cols), lambda i, j, k, blk=blk: (blk(i, j, k), 0)))
        sides.append((transposed, nblocks))
        block_bytes += 2 * (4 + 2) * slab * cols
    body = functools.partial(_mm_body, npairs=npairs, grid=grid, act=act, has_res=res is not None, inv_d=inv_d,
                             cast_w=cast_w, sides=tuple(sides), has_conv=conv is not None,
                             emit_norm=next_gain is not None)
    outs = pl.pallas_call(
        body,
        out_shape=tuple(out_shape),
        grid=grid,
        in_specs=in_specs,
        out_specs=tuple(out_specs),
        scratch_shapes=scratch,
        compiler_params=_params(("arbitrary", "arbitrary", "arbitrary"), block_bytes),
        name=name,
    )(*args)
    return outs[0] if len(outs) == 1 else outs


def _cast_transposed_body(src_ref, dst_ref):
    dst_ref[...] = src_ref[...].T.astype(dst_ref.dtype)


def _cast_transposed(w_t, layer, conv_groups):
    _, n, kdim = w_t.shape
    bn = CONV_SEG
    return pl.pallas_call(
        _cast_transposed_body,
        out_shape=jax.ShapeDtypeStruct((kdim, n), _BF16),
        grid=(pl.cdiv(n, bn),),
        in_specs=[pl.BlockSpec((None, bn, kdim), lambda t: (layer, t, 0))],
        out_specs=pl.BlockSpec((kdim, bn), lambda t: (0, _conv_group_order(t, conv_groups))),
        compiler_params=_params(("parallel",)),
        name="cast_transposed",
    )(w_t)


def _bf16_split_dot(a_exact_bf16, x):
    hi = x.astype(_BF16)
    lo = (x - hi.astype(_F32)).astype(_BF16)
    return (jnp.dot(a_exact_bf16, hi, preferred_element_type=_F32)
            + jnp.dot(a_exact_bf16, lo, preferred_element_type=_F32))


def _gla_constants(L, G):
    nlev = L.bit_length() - 1
    r = np.arange(L * G)
    cum_op = (r[None, :] <= r[:, None]) & (r[None, :] // L == r[:, None] // L)
    i = np.arange(L)
    base = [np.eye(L, dtype=bool)]
    for lev in range(nlev):
        s = L >> lev
        lower = (i % s) >= s // 2
        base.append((i[:, None] // s == i[None, :] // s) & lower[:, None] & ~lower[None, :])
    base = np.stack(base).astype(np.float32)
    masks = np.zeros((nlev + 1, 2, L, 2 * L), np.float32)
    masks[:, 0, :, :L] = base
    masks[:, 1, :, L:] = base
    return jnp.asarray(cum_op, _BF16), jnp.asarray(masks)


def _gla_level_operand(q, k, b, s, row8):
    half = s // 2
    n = b.shape[0]
    parts = []
    if half >= SUBLANES:
        for r0 in range(0, n, half):
            rows = slice(r0, r0 + half)
            r_bound = (r0 // s) * s + half - 1
            b_r = b[r_bound:r_bound + 1, :]
            if (r0 // half) % 2:
                parts.append(q[rows] * jnp.exp2(b[rows] - b_r))
            else:
                parts.append(k[rows] * jnp.exp2(b_r - b[rows]))
    else:
        low = (row8 & (s - 1)) >= half
        sign = jnp.where(low, -1.0, 1.0)
        for r0 in range(0, n, SUBLANES):
            rows = slice(r0, r0 + SUBLANES)
            b8 = b[rows]
            if half == 1:
                b_r = jnp.where(low, pltpu.roll(b8, 1, axis=0), b8)
            else:
                b_r = b8[half - 1:half, :]
                for blk in range(1, SUBLANES // s):
                    b_r = jnp.where(row8 >= blk * s, b8[blk * s + half - 1:blk * s + half, :], b_r)
            parts.append(jnp.where(low, q[rows], k[rows]) * jnp.exp2((b_r - b8) * sign))
    return jnp.concatenate(parts, axis=0).astype(_BF16)


def _gla_prepare_head(q, k, v, la, cum_op, mask_ref, row8):
    L, G = GLA_CHUNK, GLA_GROUP
    nlev = L.bit_length() - 1
    dk = q.shape[1]
    b = _bf16_split_dot(cum_op, la)
    b_last = [b[(c + 1) * L - 1:(c + 1) * L, :] for c in range(G)]
    q_in = (q * jnp.exp2(b)).astype(_BF16)
    k_dec = jnp.concatenate(
        [k[c * L:(c + 1) * L] * jnp.exp2(b_last[c] - b[c * L:(c + 1) * L]) for c in range(G)], axis=0
    ).astype(_BF16)
    decays = [jnp.exp2(bl) for bl in b_last]

    acc = [None] * G

    def add_level(idx, p):
        for c in range(G):
            t = c // 2
            blk = p[c * L:(c + 1) * L, t * LANES:(t + 1) * LANES] * mask_ref[idx, c % 2]
            acc[c] = blk if acc[c] is None else acc[c] + blk

    add_level(0, lax.dot_general(q.astype(_BF16), k.astype(_BF16), _NT, preferred_element_type=_F32))
    for lev in range(nlev):
        x = _gla_level_operand(q, k, b, L >> lev, row8)
        add_level(lev + 1, lax.dot_general(x, x, _NT, preferred_element_type=_F32))

    zeros = jnp.zeros((L, LANES), _F32)
    a_bd = jnp.concatenate(
        [jnp.concatenate([acc[c] if t == c // 2 else zeros for t in range(G // 2)], axis=1) for c in range(G)],
        axis=0).astype(_BF16)
    o_intra = jnp.dot(a_bd, v, preferred_element_type=_F32)
    return o_intra, q_in, k_dec, decays


def _gla_body(q_ref, k_ref, v_ref, g_ref, a_ref, wg_ref, bg_ref, gn_ref, cum_ref, mask_ref, o_ref,
              state_ref, oin_ref, qin_ref, kdec_ref, dec_ref, *, dk, dv):
    L, G = GLA_CHUNK, GLA_GROUP

    @pl.when(pl.program_id(1) == 0)
    def _():
        state_ref[...] = jnp.zeros_like(state_ref)

    z = jnp.dot(a_ref[...].astype(_BF16), wg_ref[...], preferred_element_type=_F32) + bg_ref[...]
    la_all = (jnp.minimum(z, 0.0) - jnp.log(1.0 + jnp.exp(-jnp.abs(z)))) * (LOG2E / GATE_TAU)

    row8 = lax.broadcasted_iota(jnp.int32, (SUBLANES, dk), 0)
    scale = dk ** -0.5
    cum_op = cum_ref[...]
    for hd in range(GLA_HEADS_PER_STEP):
        kc = slice(hd * dk, (hd + 1) * dk)
        vc = slice(hd * dv, (hd + 1) * dv)
        o_intra, q_in, k_dec, decays = _gla_prepare_head(
            q_ref[:, kc].astype(_F32) * scale, k_ref[:, kc].astype(_F32), v_ref[:, vc], la_all[:, kc],
            cum_op, mask_ref, row8)
        oin_ref[:, vc] = o_intra
        qin_ref[:, kc] = q_in
        kdec_ref[:, kc] = k_dec
        for c in range(G):
            dec_ref[c:c + 1, kc] = decays[c]

    for c in range(G):
        rows = slice(c * L, (c + 1) * L)
        for hd in range(GLA_HEADS_PER_STEP):
            kc = slice(hd * dk, (hd + 1) * dk)
            vc = slice(hd * dv, (hd + 1) * dv)
            st = state_ref[hd]
            o = oin_ref[rows, vc] + lax.dot_general(qin_ref[rows, kc], st.astype(_BF16), _NT,
                                                    preferred_element_type=_F32)
            upd = lax.dot_general(v_ref[rows, vc], kdec_ref[rows, kc], _TN, preferred_element_type=_F32)
            state_ref[hd] = st * dec_ref[c:c + 1, kc] + upd
            o = o * lax.rsqrt(jnp.mean(o * o, axis=-1, keepdims=True) + EPS)
            g = g_ref[rows, vc].astype(_F32)
            o_ref[rows, vc] = (o * gn_ref[:, vc] * (g * jax.nn.sigmoid(g))).astype(o_ref.dtype)


def _gla(proj, a_lr, w_gate_pad, b_gate, gla_norm, *, dk, dv):
    s = proj.shape[0]
    hp = GLA_HEADS_PER_STEP
    tb = GLA_CHUNK * GLA_GROUP
    kw, vw = hp * dk, hp * dv
    q_blk = 0
    k_blk = q_blk + GLA_HEADS // hp
    v_blk = (2 * GLA_HEADS * dk) // vw
    g_blk = v_blk + GLA_HEADS // hp
    assert v_blk * vw == 2 * GLA_HEADS * dk and proj.shape[1] == 2 * GLA_HEADS * (dk + dv)
    assert 2 * GLA_CHUNK == LANES and GLA_GROUP % 2 == 0 and s % tb == 0 and GLA_HEADS % hp == 0
    na = a_lr.shape[1]
    cum_op, pair_mask = _gla_constants(GLA_CHUNK, GLA_GROUP)
    body = functools.partial(_gla_body, dk=dk, dv=dv)
    return pl.pallas_call(
        body,
        out_shape=jax.ShapeDtypeStruct((s, GLA_HEADS * dv), _BF16),
        grid=(GLA_HEADS // hp, s // tb),
        in_specs=[pl.BlockSpec((tb, kw), lambda hg, t: (t, q_blk + hg)),
                  pl.BlockSpec((tb, kw), lambda hg, t: (t, k_blk + hg)),
                  pl.BlockSpec((tb, vw), lambda hg, t: (t, v_blk + hg)),
                  pl.BlockSpec((tb, vw), lambda hg, t: (t, g_blk + hg)),
                  pl.BlockSpec((tb, na), lambda hg, t: (t, 0)),
                  pl.BlockSpec((na, kw), lambda hg, t: (0, hg)),
                  pl.BlockSpec((1, kw), lambda hg, t: (0, hg)),
                  pl.BlockSpec((1, vw), lambda hg, t: (0, hg)),
                  pl.BlockSpec(cum_op.shape, lambda hg, t: (0, 0)),
                  pl.BlockSpec(pair_mask.shape, lambda hg, t: (0, 0, 0, 0))],
        out_specs=pl.BlockSpec((tb, vw), lambda hg, t: (t, hg)),
        scratch_shapes=[pltpu.VMEM((hp, dv, dk), _F32),
                        pltpu.VMEM((tb, vw), _F32),
                        pltpu.VMEM((tb, kw), _BF16),
                        pltpu.VMEM((tb, kw), _BF16),
                        pltpu.VMEM((SUBLANES, kw), _F32)],
        compiler_params=_params(("parallel", "arbitrary")),
        name="gla",
    )(proj, proj, proj, proj, a_lr, w_gate_pad, b_gate.reshape(1, -1).astype(_F32),
      gla_norm.reshape(1, -1).astype(_F32), cum_op, pair_mask)


def _xattn_body(res_ref, gin_ref, wq_ref, k_ref, v_ref, wo_ref, gain_ref, o_ref, hgo_ref, ssqo_ref, *, dh):
    scale = dh ** -0.5
    x = res_ref[...]
    r = lax.rsqrt(jnp.mean(x * x, axis=-1, keepdims=True) + EPS)
    q = jnp.dot((x * gin_ref[...]).astype(_BF16), wq_ref[...], preferred_element_type=_F32) * r
    q = q.astype(_BF16)
    heads = []
    for hd in range(XATTN_HEADS):
        sl = slice(hd * dh, (hd + 1) * dh)
        sc = lax.dot_general(q[:, sl], k_ref[:, sl], _NT, preferred_element_type=_F32) * scale
        p = jnp.exp(sc - jnp.max(sc, axis=-1, keepdims=True))
        denom = jnp.sum(p, axis=-1, keepdims=True)
        pv = jnp.dot(p.astype(_BF16), v_ref[:, sl], preferred_element_type=_F32)
        heads.append((pv / denom).astype(_BF16))
    att = jnp.concatenate(heads, axis=1)
    h = jnp.dot(att, wo_ref[...], preferred_element_type=_F32) + x
    o_ref[...] = h
    hgo_ref[...] = (h * gain_ref[...]).astype(hgo_ref.dtype)
    ssqo_ref[...] = _lane_partial_ssq(h)


def _xattn(res, gain, wq, wo, k, v, next_gain, *, bm=256):
    s, d = res.shape
    mlen, width = k.shape
    body = functools.partial(_xattn_body, dh=width // XATTN_HEADS)
    row_block = lambda w: pl.BlockSpec((bm, w), lambda i: (i, 0))
    whole = lambda a: pl.BlockSpec(a.shape, lambda i: (0, 0))
    w_block = lambda a: pl.BlockSpec(a.shape, lambda i: (0, 0), pipeline_mode=pl.Buffered(1))
    gain_row = pl.BlockSpec((1, d), lambda i: (0, 0))
    block_bytes = (2 * bm * 4 * d + 2 * bm * (4 * d + 2 * d + 4 * LANES)
                   + 2 * 2 * d * width + 2 * 2 * 2 * mlen * width)
    return pl.pallas_call(
        body,
        out_shape=(jax.ShapeDtypeStruct((s, d), _F32), jax.ShapeDtypeStruct((s, d), _BF16),
                   jax.ShapeDtypeStruct((s, LANES), _F32)),
        grid=(s // bm,),
        in_specs=[row_block(d), gain_row, w_block(wq), whole(k), whole(v), w_block(wo), gain_row],
        out_specs=(row_block(d), row_block(d), row_block(LANES)),
        compiler_params=_params(("parallel",), block_bytes),
        name="xattn",
    )(res, gain.reshape(1, d).astype(_F32), wq, k, v, wo, next_gain.reshape(1, d).astype(_F32))


def kernel(x, mem, norm_mix, w_in, w_conv, w_gate_up, b_gate, gla_norm, w_out, norm_xattn, norm_mem,
           wq_x, wk_x, wv_x, wo_x, norm_mlp, w_up, w_down, norm_final):
    bsz, seq, d = x.shape
    depth = w_in.shape[0]
    conv_width = w_conv.shape[2]
    gla_key = w_gate_up.shape[2]
    gla_width = gla_norm.shape[1]
    gate_rank = w_gate_up.shape[1]
    dk = gla_key // GLA_HEADS
    dv = gla_width // GLA_HEADS
    main_cols = 3 * conv_width + 2 * gla_key + 2 * gla_width
    assert bsz == 1 and w_in.shape[2] == main_cols + gate_rank

    w_gate_all = jnp.zeros((depth, LANES, gla_key), _BF16).at[:, :gate_rank].set(w_gate_up.astype(_BF16))
    assert conv_width % CONV_SEG == 0 and (main_cols - 3 * conv_width) % (3 * CONV_SEG) == 0
    conv_groups = conv_width // CONV_SEG
    w_in_t = jnp.swapaxes(w_in, 1, 2)
    w_in_l = _cast_transposed(w_in_t, 0, conv_groups)

    h = x.reshape(seq, d)
    mem2 = mem.reshape(mem.shape[1], d)
    hg, ssq = _norm_prep(h, norm_mix[0])
    for l in range(depth):
        w_a = jnp.zeros((d, LANES), _BF16).at[:, :gate_rank].set(w_in_l[:, main_cols:])
        w_taps = jnp.zeros((SUBLANES, conv_width), _F32).at[:CONV_K].set(w_conv[l].astype(_F32))
        bn_in = 3 * CONV_SEG
        y_conv, a_lr, w_up_l, w_out_l = _matmul(
            [hg], [(w_in_l, None, 0)], out_dtype=_BF16, row_ssq=ssq, conv=(w_taps, w_a), bn=bn_in,
            cols=(0, conv_groups), side_casts=[(w_up, l, None), (w_out, l, None)], name="in_proj_conv")
        proj, wq_l, wo_l, *w_in_next = _matmul(
            [hg], [(w_in_l, None, 0)], out_dtype=_BF16, row_ssq=ssq, bn=bn_in,
            cols=(conv_groups, (main_cols - 3 * conv_width) // bn_in), name="in_proj_gla",
            side_casts=[(wq_x, l, None), (wo_x, l, None)] + ([(w_in_t, l + 1, conv_groups)] if l + 1 < depth else []))
        o_gla = _gla(proj, a_lr, w_gate_all[l], b_gate[l], gla_norm[l], dk=dk, dv=dv)
        h = _matmul([y_conv, o_gla], [(w_out_l, None, 0), (w_out_l, None, conv_width)], out_dtype=_F32, res=h,
                    bk=conv_width, name="out_proj")
        memn = _rmsnorm(mem2, norm_mem[l], _BF16, 256)
        k = _matmul([memn], [(wk_x, l, 0)], out_dtype=_BF16, bn=512, name="xattn_k")
        v = _matmul([memn], [(wv_x, l, 0)], out_dtype=_BF16, bn=512, name="xattn_v")
        h, hg, ssq = _xattn(h, norm_xattn[l], wq_l, wo_l, k, v, norm_mlp[l])
        hid, w_down_l = _matmul([hg], [(w_up_l, None, 0)], out_dtype=_BF16, act="relu2", row_ssq=ssq,
                                side_casts=[(w_down, l, None)], name="mlp_up")
        if l + 1 < depth:
            h, hg, ssq = _matmul([hid], [(w_down_l, None, 0)], out_dtype=_F32, res=h, next_gain=norm_mix[l + 1],
                                 name="mlp_down")
            w_in_l = w_in_next[0]
        else:
            h = _matmul([hid], [(w_down_l, None, 0)], out_dtype=_F32, res=h, name="mlp_down")
    out = _rmsnorm(h, norm_final, _F32, 512)
    return out.reshape(bsz, seq, d)
```

```python
import functools
import math

import jax
import jax.numpy as jnp
import numpy as np
from jax import lax
from jax.experimental import pallas as pl
from jax.experimental.pallas import tpu as pltpu

EPS = 1e-6
GLA_CHUNK = 64
GLA_GROUP = 4
GLA_HEADS = 4
GLA_HEADS_PER_STEP = 4
XATTN_HEADS = 4
GATE_TAU = 16.0
CONV_K = 3
MXU_COLS = 256
CONV_SEG = MXU_COLS
LANES = 128
SUBLANES = 8
MIB = 1024 * 1024
VMEM_CAP = 60 * MIB
VMEM_TEMPS = 12 * MIB
VMEM_DEFAULT = 40 * MIB
LOG2E = math.log2(math.e)

_BF16 = jnp.bfloat16
_F32 = jnp.float32
_NT = (((1,), (1,)), ((), ()))
_TN = (((0,), (0,)), ((), ()))


def _params(sem, block_bytes=None):
    limit = VMEM_DEFAULT if block_bytes is None else min(VMEM_CAP, block_bytes + VMEM_TEMPS)
    return pltpu.CompilerParams(dimension_semantics=sem, vmem_limit_bytes=limit)


def _lane_partial_ssq(h):
    sq = h * h
    acc = sq[:, :LANES]
    for t in range(1, h.shape[1] // LANES):
        acc = acc + sq[:, t * LANES:(t + 1) * LANES]
    return acc


def _row_rsqrt(ssq_ref, inv_d):
    return lax.rsqrt(jnp.sum(ssq_ref[...], axis=-1, keepdims=True) * inv_d + EPS)


def _rmsnorm_body(x_ref, g_ref, o_ref):
    x = x_ref[...]
    y = x * lax.rsqrt(jnp.mean(x * x, axis=-1, keepdims=True) + EPS)
    o_ref[...] = (y * g_ref[...]).astype(o_ref.dtype)


def _rmsnorm(x, gain, out_dtype, bm):
    m, d = x.shape
    bm = min(bm, m)
    return pl.pallas_call(
        _rmsnorm_body,
        out_shape=jax.ShapeDtypeStruct((m, d), out_dtype),
        grid=(m // bm,),
        in_specs=[pl.BlockSpec((bm, d), lambda i: (i, 0)), pl.BlockSpec((1, d), lambda i: (0, 0))],
        out_specs=pl.BlockSpec((bm, d), lambda i: (i, 0)),
        compiler_params=_params(("parallel",)),
        name="rmsnorm",
    )(x, gain.reshape(1, d).astype(_F32))


def _norm_prep_body(x_ref, g_ref, hg_ref, ssq_ref):
    x = x_ref[...]
    hg_ref[...] = (x * g_ref[...]).astype(hg_ref.dtype)
    ssq_ref[...] = _lane_partial_ssq(x)


def _norm_prep(x, gain, bm=512):
    m, d = x.shape
    row_block = lambda width: pl.BlockSpec((bm, width), lambda i: (i, 0))
    return pl.pallas_call(
        _norm_prep_body,
        out_shape=(jax.ShapeDtypeStruct((m, d), _BF16), jax.ShapeDtypeStruct((m, LANES), _F32)),
        grid=(m // bm,),
        in_specs=[row_block(d), pl.BlockSpec((1, d), lambda i: (0, 0))],
        out_specs=(row_block(d), row_block(LANES)),
        compiler_params=_params(("parallel",)),
        name="norm_prep",
    )(x, gain.reshape(1, d).astype(_F32))


def _mm_body(*refs, npairs, grid, act, has_res, inv_d, cast_w, sides, has_conv, emit_norm):
    it = iter(refs)
    xs = [next(it) for _ in range(npairs)]
    ws = [next(it) for _ in range(npairs)]
    res_ref = next(it) if has_res else None
    ssq_in_ref = next(it) if inv_d is not None else None
    wconv_ref, wgate_ref = (next(it), next(it)) if has_conv else (None, None)
    gain_ref = next(it) if emit_norm else None
    side_src = [next(it) for _ in sides]
    o_ref = next(it)
    gate_ref = next(it) if has_conv else None
    hg_ref, ssq_ref = (next(it), next(it)) if emit_norm else (None, None)
    side_dst = [next(it) for _ in sides]
    tail_ref = next(it) if has_conv else None
    nk = grid[2]

    step = (pl.program_id(0) * grid[1] + pl.program_id(1)) * nk + pl.program_id(2)
    for (transposed, nblocks), src, dst in zip(sides, side_src, side_dst):
        def cast(src=src, dst=dst, transposed=transposed):
            val = src[...]
            dst[...] = (val.T if transposed else val).astype(dst.dtype)

        if nblocks == grid[0] * grid[1] * nk:
            cast()
        else:
            pl.when(step < nblocks)(cast)

    bn = ws[0].shape[1]
    chunk = min(MXU_COLS, bn)

    def product(c0):
        cs = slice(c0, c0 + chunk)
        acc = None
        for x_ref, w_ref in zip(xs, ws):
            w = w_ref[:, cs]
            part = jnp.dot(x_ref[...], w.astype(_BF16) if cast_w else w, preferred_element_type=_F32)
            acc = part if acc is None else acc + part
        return acc

    if nk > 1:
        @pl.when(pl.program_id(2) == 0)
        def _():
            o_ref[...] = res_ref[...] if has_res else jnp.zeros_like(o_ref)

        def accumulate():
            for c0 in range(0, bn, chunk):
                o_ref[:, c0:c0 + chunk] += product(c0)

        if not emit_norm:
            accumulate()
            return

        def finish():
            partial = None
            for c0 in range(0, bn, chunk):
                cs = slice(c0, c0 + chunk)
                h = o_ref[:, cs] + product(c0)
                o_ref[:, cs] = h
                hg_ref[:, cs] = (h * gain_ref[:, cs]).astype(hg_ref.dtype)
                part = _lane_partial_ssq(h)
                partial = part if partial is None else partial + part
            first = pl.program_id(1) == 0

            @pl.when(first)
            def _():
                ssq_ref[...] = partial

            @pl.when(jnp.logical_not(first))
            def _():
                ssq_ref[...] += partial

        last = pl.program_id(2) == nk - 1
        pl.when(jnp.logical_not(last))(accumulate)
        pl.when(last)(finish)
        return

    row_scale = _row_rsqrt(ssq_in_ref, inv_d) if inv_d is not None else None

    def finished(c0):
        acc = product(c0)
        if row_scale is not None:
            acc = acc * row_scale
        if act == "relu2":
            r = jnp.maximum(acc, 0.0)
            acc = r * r
        if has_res:
            acc = acc + res_ref[:, c0:c0 + chunk]
        return acc

    if not has_conv:
        for c0 in range(0, bn, chunk):
            o_ref[:, c0:c0 + chunk] = finished(c0).astype(o_ref.dtype)
        return

    j = pl.program_id(1)

    @pl.when(step == 0)
    def _():
        tail_ref[...] = jnp.zeros_like(tail_ref)

    @pl.when(j == 0)
    def _():
        gate_ref[...] = jnp.dot(xs[0][...], wgate_ref[...], preferred_element_type=_F32) * row_scale

    b_gate, c_gate, h_in = (finished(c0) for c0 in range(0, bn, chunk))
    u = c_gate * h_in
    rows = u.shape[0]
    row = lax.broadcasted_iota(jnp.int32, u.shape, 0)
    tail = tail_ref[j]
    prev1 = tail[SUBLANES - 1:SUBLANES, :]
    prev2 = tail[SUBLANES - 2:SUBLANES - 1, :]
    u1 = jnp.where(row == 0, prev1, pltpu.roll(u, 1, axis=0))
    u2 = jnp.where(row == 0, prev2, jnp.where(row == 1, prev1, pltpu.roll(u, 2, axis=0)))
    w = wconv_ref[...]
    y = b_gate * (w[0:1, :] * u2 + w[1:2, :] * u1 + w[2:3, :] * u)
    o_ref[...] = y.astype(o_ref.dtype)
    tail_ref[j] = u[rows - SUBLANES:, :]


def _conv_group_order(blk, nseg):
    return jnp.where(blk < 3 * nseg, 3 * (blk % nseg) + blk // nseg, blk)


def _matmul(xs, ws, *, out_dtype, res=None, act=None, row_ssq=None, next_gain=None, side_casts=(), conv=None,
            cols=None, bm=1024, bn=1024, bk=4096, name="matmul"):
    cast_w = ws[0][0].dtype == _F32
    w_bytes = 4 if cast_w else 2
    npairs = len(xs)
    m, kdim = xs[0].shape
    n = ws[0][0].shape[-1]
    bm, bn, bk = min(bm, m), min(bn, n), min(bk, kdim)
    nk = kdim // bk
    assert m % bm == 0 and kdim % bk == 0
    assert nk == 1 or (act is None and row_ssq is None and out_dtype == _F32)
    assert next_gain is None or nk > 1
    j0, ntiles = cols if cols is not None else (0, n // bn)
    grid = (m // bm, ntiles, nk)
    nsteps = grid[0] * grid[1] * nk
    in_specs = [pl.BlockSpec((bm, bk), lambda i, j, k: (i, k))] * npairs
    for w, layer, row0 in ws:
        assert row0 % bk == 0
        kb = row0 // bk
        if layer is None:
            in_specs.append(pl.BlockSpec((bk, bn), lambda i, j, k, kb=kb: (kb + k, j0 + j)))
        else:
            in_specs.append(pl.BlockSpec((None, bk, bn), lambda i, j, k, kb=kb, layer=layer: (layer, kb + k, j0 + j)))
    args = list(xs) + [w for w, _, _ in ws]
    out_bytes = jnp.dtype(out_dtype).itemsize
    block_bytes = 2 * (npairs * (2 * bm * bk + w_bytes * bk * bn) + bm * bn * out_bytes)
    if res is not None:
        in_specs.append(pl.BlockSpec((bm, bn), lambda i, j, k: (i, j)))
        args.append(res)
        block_bytes += 2 * 4 * bm * bn
    inv_d = None
    if row_ssq is not None:
        in_specs.append(pl.BlockSpec((bm, LANES), lambda i, j, k: (i, 0)))
        args.append(row_ssq)
        inv_d = 1.0 / kdim
        block_bytes += 2 * 4 * bm * LANES
    scratch = []
    if conv is None:
        out_shape = [jax.ShapeDtypeStruct((m, ntiles * bn), out_dtype)]
        out_specs = [pl.BlockSpec((bm, bn), lambda i, j, k: (i, j))]
    else:
        taps, w_gate = conv
        seg = bn // 3
        assert seg == CONV_SEG and nk == 1 and taps.shape[1] == ntiles * seg and row_ssq is not None
        in_specs += [pl.BlockSpec((SUBLANES, seg), lambda i, j, k: (0, j)),
                     pl.BlockSpec(w_gate.shape, lambda i, j, k: (0, 0))]
        args += [taps, w_gate]
        out_shape = [jax.ShapeDtypeStruct((m, ntiles * seg), out_dtype),
                     jax.ShapeDtypeStruct((m, w_gate.shape[1]), _F32)]
        out_specs = [pl.BlockSpec((bm, seg), lambda i, j, k: (i, j)),
                     pl.BlockSpec((bm, w_gate.shape[1]), lambda i, j, k: (i, 0))]
        scratch = [pltpu.VMEM((ntiles, SUBLANES, seg), _F32)]
        block_bytes += 2 * (2 * w_gate.size + 4 * bm * w_gate.shape[1])
    if next_gain is not None:
        assert n == ntiles * bn
        in_specs.append(pl.BlockSpec((1, bn), lambda i, j, k: (0, j)))
        args.append(next_gain.reshape(1, n).astype(_F32))
        out_shape += [jax.ShapeDtypeStruct((m, n), _BF16), jax.ShapeDtypeStruct((m, LANES), _F32)]
        out_specs += [pl.BlockSpec((bm, bn), lambda i, j, k: (i, j)),
                      pl.BlockSpec((bm, LANES), lambda i, j, k: (i, 0))]
        block_bytes += 2 * (2 * bm * bn + 4 * bm * LANES)
    sides = []
    for src, layer, conv_groups in side_casts:
        _, rows, cols = src.shape
        transposed = conv_groups is not None
        if transposed:
            slab = CONV_SEG
            nblocks = pl.cdiv(rows, slab)
            out_shape.append(jax.ShapeDtypeStruct((cols, rows), _BF16))
        else:
            tile = 2 * SUBLANES
            slab = next(c for c in range(tile, rows + 1, tile) if rows % c == 0 and rows // c <= nsteps)
            nblocks = rows // slab
            out_shape.append(jax.ShapeDtypeStruct((rows, cols), _BF16))
        assert nblocks <= nsteps
        blk = lambda i, j, k, last=nblocks - 1: jnp.minimum((i * grid[1] + j) * nk + k, last)
        in_specs.append(pl.BlockSpec((None, slab, cols), lambda i, j, k, layer=layer, blk=blk: (layer, blk(i, j, k), 0)))
        args.append(src)
        if transposed:
            out_specs.append(pl.BlockSpec(
                (cols, slab), lambda i, j, k, blk=blk, g=conv_groups: (0, _conv_group_order(blk(i, j, k), g))))
        else:
            out_specs.append(pl.BlockSpec((slab, cols), lambda i, j, k, blk=blk: (blk(i, j, k), 0)))
        sides.append((transposed, nblocks))
        block_bytes += 2 * (4 + 2) * slab * cols
    body = functools.partial(_mm_body, npairs=npairs, grid=grid, act=act, has_res=res is not None, inv_d=inv_d,
                             cast_w=cast_w, sides=tuple(sides), has_conv=conv is not None,
                             emit_norm=next_gain is not None)
    outs = pl.pallas_call(
        body,
        out_shape=tuple(out_shape),
        grid=grid,
        in_specs=in_specs,
        out_specs=tuple(out_specs),
        scratch_shapes=scratch,
        compiler_params=_params(("arbitrary", "arbitrary", "arbitrary"), block_bytes),
        name=name,
    )(*args)
    return outs[0] if len(outs) == 1 else outs


def _cast_transposed_body(src_ref, dst_ref):
    dst_ref[...] = src_ref[...].T.astype(dst_ref.dtype)


def _cast_transposed(w_t, layer, conv_groups):
    _, n, kdim = w_t.shape
    bn = CONV_SEG
    return pl.pallas_call(
        _cast_transposed_body,
        out_shape=jax.ShapeDtypeStruct((kdim, n), _BF16),
        grid=(pl.cdiv(n, bn),),
        in_specs=[pl.BlockSpec((None, bn, kdim), lambda t: (layer, t, 0))],
        out_specs=pl.BlockSpec((kdim, bn), lambda t: (0, _conv_group_order(t, conv_groups))),
        compiler_params=_params(("parallel",)),
        name="cast_transposed",
    )(w_t)


def _bf16_split_dot(a_exact_bf16, x):
    hi = x.astype(_BF16)
    lo = (x - hi.astype(_F32)).astype(_BF16)
    return (jnp.dot(a_exact_bf16, hi, preferred_element_type=_F32)
            + jnp.dot(a_exact_bf16, lo, preferred_element_type=_F32))


def _gla_constants(L, G):
    nlev = L.bit_length() - 1
    r = np.arange(L * G)
    cum_op = (r[None, :] <= r[:, None]) & (r[None, :] // L == r[:, None] // L)
    i = np.arange(L)
    base = [np.eye(L, dtype=bool)]
    for lev in range(nlev):
        s = L >> lev
        lower = (i % s) >= s // 2
        base.append((i[:, None] // s == i[None, :] // s) & lower[:, None] & ~lower[None, :])
    base = np.stack(base).astype(np.float32)
    masks = np.zeros((nlev + 1, 2, L, 2 * L), np.float32)
    masks[:, 0, :, :L] = base
    masks[:, 1, :, L:] = base
    return jnp.asarray(cum_op, _BF16), jnp.asarray(masks)


def _gla_level_operand(q, k, b, la, s, row8):
    half = s // 2
    n = b.shape[0]
    parts = []
    if half == 1:
        low = (row8 & 1) == 1
        q_dec = q * jnp.exp2(la)
        for r0 in range(0, n, SUBLANES):
            rows = slice(r0, r0 + SUBLANES)
            parts.append(jnp.where(low, q_dec[rows], k[rows]))
    elif half >= SUBLANES:
        for r0 in range(0, n, half):
            rows = slice(r0, r0 + half)
            r_bound = (r0 // s) * s + half - 1
            b_r = b[r_bound:r_bound + 1, :]
            if (r0 // half) % 2:
                parts.append(q[rows] * jnp.exp2(b[rows] - b_r))
            else:
                parts.append(k[rows] * jnp.exp2(b_r - b[rows]))
    else:
        low = (row8 & (s - 1)) >= half
        sign = jnp.where(low, -1.0, 1.0)
        for r0 in range(0, n, SUBLANES):
            rows = slice(r0, r0 + SUBLANES)
            b8 = b[rows]
            b_r = b8[half - 1:half, :]
            for blk in range(1, SUBLANES // s):
                b_r = jnp.where(row8 >= blk * s, b8[blk * s + half - 1:blk * s + half, :], b_r)
            parts.append(jnp.where(low, q[rows], k[rows]) * jnp.exp2((b_r - b8) * sign))
    return jnp.concatenate(parts, axis=0).astype(_BF16)


def _gla_prepare_head(q, k, v, la, cum_op, mask_ref, row8):
    L, G = GLA_CHUNK, GLA_GROUP
    nlev = L.bit_length() - 1
    dk = q.shape[1]
    b = _bf16_split_dot(cum_op, la)
    b_last = [b[(c + 1) * L - 1:(c + 1) * L, :] for c in range(G)]
    q_in = (q * jnp.exp2(b)).astype(_BF16)
    k_dec = jnp.concatenate(
        [k[c * L:(c + 1) * L] * jnp.exp2(b_last[c] - b[c * L:(c + 1) * L]) for c in range(G)], axis=0
    ).astype(_BF16)
    decays = [jnp.exp2(bl) for bl in b_last]

    acc = [None] * G

    def add_level(idx, p):
        for c in range(G):
            t = c // 2
            blk = p[c * L:(c + 1) * L, t * LANES:(t + 1) * LANES] * mask_ref[idx, c % 2]
            acc[c] = blk if acc[c] is None else acc[c] + blk

    add_level(0, lax.dot_general(q.astype(_BF16), k.astype(_BF16), _NT, preferred_element_type=_F32))
    for lev in range(nlev):
        x = _gla_level_operand(q, k, b, la, L >> lev, row8)
        add_level(lev + 1, lax.dot_general(x, x, _NT, preferred_element_type=_F32))

    zeros = jnp.zeros((L, LANES), _F32)
    a_bd = jnp.concatenate(
        [jnp.concatenate([acc[c] if t == c // 2 else zeros for t in range(G // 2)], axis=1) for c in range(G)],
        axis=0).astype(_BF16)
    o_intra = jnp.dot(a_bd, v, preferred_element_type=_F32)
    return o_intra, q_in, k_dec, decays


def _gla_body(q_ref, k_ref, v_ref, g_ref, a_ref, wg_ref, bg_ref, gn_ref, cum_ref, mask_ref, o_ref,
              state_ref, oin_ref, qin_ref, kdec_ref, dec_ref, *, dk, dv):
    L, G = GLA_CHUNK, GLA_GROUP

    @pl.when(pl.program_id(1) == 0)
    def _():
        state_ref[...] = jnp.zeros_like(state_ref)

    z = jnp.dot(a_ref[...].astype(_BF16), wg_ref[...], preferred_element_type=_F32) + bg_ref[...]
    la_all = (jnp.minimum(z, 0.0) - jnp.log(1.0 + jnp.exp(-jnp.abs(z)))) * (LOG2E / GATE_TAU)

    row8 = lax.broadcasted_iota(jnp.int32, (SUBLANES, dk), 0)
    scale = dk ** -0.5
    cum_op = cum_ref[...]
    for hd in range(GLA_HEADS_PER_STEP):
        kc = slice(hd * dk, (hd + 1) * dk)
        vc = slice(hd * dv, (hd + 1) * dv)
        o_intra, q_in, k_dec, decays = _gla_prepare_head(
            q_ref[:, kc].astype(_F32) * scale, k_ref[:, kc].astype(_F32), v_ref[:, vc], la_all[:, kc],
            cum_op, mask_ref, row8)
        oin_ref[:, vc] = o_intra
        qin_ref[:, kc] = q_in
        kdec_ref[:, kc] = k_dec
        for c in range(G):
            dec_ref[c:c + 1, kc] = decays[c]

    for c in range(G):
        rows = slice(c * L, (c + 1) * L)
        for hd in range(GLA_HEADS_PER_STEP):
            kc = slice(hd * dk, (hd + 1) * dk)
            vc = slice(hd * dv, (hd + 1) * dv)
            st = state_ref[hd]
            o = oin_ref[rows, vc] + lax.dot_general(qin_ref[rows, kc], st.astype(_BF16), _NT,
                                                    preferred_element_type=_F32)
            upd = lax.dot_general(v_ref[rows, vc], kdec_ref[rows, kc], _TN, preferred_element_type=_F32)
            state_ref[hd] = st * dec_ref[c:c + 1, kc] + upd
            o = o * lax.rsqrt(jnp.mean(o * o, axis=-1, keepdims=True) + EPS)
            g = g_ref[rows, vc].astype(_F32)
            o_ref[rows, vc] = (o * gn_ref[:, vc] * (g * jax.nn.sigmoid(g))).astype(o_ref.dtype)


def _gla(proj, a_lr, w_gate_pad, b_gate, gla_norm, *, dk, dv):
    s = proj.shape[0]
    hp = GLA_HEADS_PER_STEP
    tb = GLA_CHUNK * GLA_GROUP
    kw, vw = hp * dk, hp * dv
    q_blk = 0
    k_blk = q_blk + GLA_HEADS // hp
    v_blk = (2 * GLA_HEADS * dk) // vw
    g_blk = v_blk + GLA_HEADS // hp
    assert v_blk * vw == 2 * GLA_HEADS * dk and proj.shape[1] == 2 * GLA_HEADS * (dk + dv)
    assert 2 * GLA_CHUNK == LANES and GLA_GROUP % 2 == 0 and s % tb == 0 and GLA_HEADS % hp == 0
    na = a_lr.shape[1]
    cum_op, pair_mask = _gla_constants(GLA_CHUNK, GLA_GROUP)
    body = functools.partial(_gla_body, dk=dk, dv=dv)
    return pl.pallas_call(
        body,
        out_shape=jax.ShapeDtypeStruct((s, GLA_HEADS * dv), _BF16),
        grid=(GLA_HEADS // hp, s // tb),
        in_specs=[pl.BlockSpec((tb, kw), lambda hg, t: (t, q_blk + hg)),
                  pl.BlockSpec((tb, kw), lambda hg, t: (t, k_blk + hg)),
                  pl.BlockSpec((tb, vw), lambda hg, t: (t, v_blk + hg)),
                  pl.BlockSpec((tb, vw), lambda hg, t: (t, g_blk + hg)),
                  pl.BlockSpec((tb, na), lambda hg, t: (t, 0)),
                  pl.BlockSpec((na, kw), lambda hg, t: (0, hg)),
                  pl.BlockSpec((1, kw), lambda hg, t: (0, hg)),
                  pl.BlockSpec((1, vw), lambda hg, t: (0, hg)),
                  pl.BlockSpec(cum_op.shape, lambda hg, t: (0, 0)),
                  pl.BlockSpec(pair_mask.shape, lambda hg, t: (0, 0, 0, 0))],
        out_specs=pl.BlockSpec((tb, vw), lambda hg, t: (t, hg)),
        scratch_shapes=[pltpu.VMEM((hp, dv, dk), _F32),
                        pltpu.VMEM((tb, vw), _F32),
                        pltpu.VMEM((tb, kw), _BF16),
                        pltpu.VMEM((tb, kw), _BF16),
                        pltpu.VMEM((SUBLANES, kw), _F32)],
        compiler_params=_params(("parallel", "arbitrary")),
        name="gla",
    )(proj, proj, proj, proj, a_lr, w_gate_pad, b_gate.reshape(1, -1).astype(_F32),
      gla_norm.reshape(1, -1).astype(_F32), cum_op, pair_mask)


def _xattn_body(res_ref, gin_ref, wq_ref, k_ref, v_ref, wo_ref, gain_ref, o_ref, hgo_ref, ssqo_ref, *, dh):
    scale = dh ** -0.5
    x = res_ref[...]
    r = lax.rsqrt(jnp.mean(x * x, axis=-1, keepdims=True) + EPS)
    q = jnp.dot((x * gin_ref[...]).astype(_BF16), wq_ref[...], preferred_element_type=_F32) * r
    q = q.astype(_BF16)
    heads = []
    for hd in range(XATTN_HEADS):
        sl = slice(hd * dh, (hd + 1) * dh)
        sc = lax.dot_general(q[:, sl], k_ref[:, sl], _NT, preferred_element_type=_F32) * scale
        p = jnp.exp(sc - jnp.max(sc, axis=-1, keepdims=True))
        denom = jnp.sum(p, axis=-1, keepdims=True)
        pv = jnp.dot(p.astype(_BF16), v_ref[:, sl], preferred_element_type=_F32)
        heads.append((pv / denom).astype(_BF16))
    att = jnp.concatenate(heads, axis=1)
    h = jnp.dot(att, wo_ref[...], preferred_element_type=_F32) + x
    o_ref[...] = h
    hgo_ref[...] = (h * gain_ref[...]).astype(hgo_ref.dtype)
    ssqo_ref[...] = _lane_partial_ssq(h)


def _xattn(res, gain, wq, wo, k, v, next_gain, *, bm=256):
    s, d = res.shape
    mlen, width = k.shape
    body = functools.partial(_xattn_body, dh=width // XATTN_HEADS)
    row_block = lambda w: pl.BlockSpec((bm, w), lambda i: (i, 0))
    whole = lambda a: pl.BlockSpec(a.shape, lambda i: (0, 0))
    w_block = lambda a: pl.BlockSpec(a.shape, lambda i: (0, 0), pipeline_mode=pl.Buffered(1))
    gain_row = pl.BlockSpec((1, d), lambda i: (0, 0))
    block_bytes = (2 * bm * 4 * d + 2 * bm * (4 * d + 2 * d + 4 * LANES)
                   + 2 * 2 * d * width + 2 * 2 * 2 * mlen * width)
    return pl.pallas_call(
        body,
        out_shape=(jax.ShapeDtypeStruct((s, d), _F32), jax.ShapeDtypeStruct((s, d), _BF16),
                   jax.ShapeDtypeStruct((s, LANES), _F32)),
        grid=(s // bm,),
        in_specs=[row_block(d), gain_row, w_block(wq), whole(k), whole(v), w_block(wo), gain_row],
        out_specs=(row_block(d), row_block(d), row_block(LANES)),
        compiler_params=_params(("parallel",), block_bytes),
        name="xattn",
    )(res, gain.reshape(1, d).astype(_F32), wq, k, v, wo, next_gain.reshape(1, d).astype(_F32))


def kernel(x, mem, norm_mix, w_in, w_conv, w_gate_up, b_gate, gla_norm, w_out, norm_xattn, norm_mem,
           wq_x, wk_x, wv_x, wo_x, norm_mlp, w_up, w_down, norm_final):
    bsz, seq, d = x.shape
    depth = w_in.shape[0]
    conv_width = w_conv.shape[2]
    gla_key = w_gate_up.shape[2]
    gla_width = gla_norm.shape[1]
    gate_rank = w_gate_up.shape[1]
    dk = gla_key // GLA_HEADS
    dv = gla_width // GLA_HEADS
    main_cols = 3 * conv_width + 2 * gla_key + 2 * gla_width
    assert bsz == 1 and w_in.shape[2] == main_cols + gate_rank

    w_gate_all = jnp.zeros((depth, LANES, gla_key), _BF16).at[:, :gate_rank].set(w_gate_up.astype(_BF16))
    assert conv_width % CONV_SEG == 0
    conv_groups = conv_width // CONV_SEG
    w_in_t = jnp.swapaxes(w_in, 1, 2)
    w_in_l = _cast_transposed(w_in_t, 0, conv_groups)

    h = x.reshape(seq, d)
    mem2 = mem.reshape(mem.shape[1], d)
    hg, ssq = _norm_prep(h, norm_mix[0])
    for l in range(depth):
        w_a = jnp.zeros((d, LANES), _BF16).at[:, :gate_rank].set(w_in_l[:, main_cols:])
        w_taps = jnp.zeros((SUBLANES, conv_width), _F32).at[:CONV_K].set(w_conv[l].astype(_F32))
        y_conv, a_lr, w_up_l, w_out_l = _matmul(
            [hg], [(w_in_l, None, 0)], out_dtype=_BF16, row_ssq=ssq, conv=(w_taps, w_a), bn=3 * CONV_SEG,
            cols=(0, conv_groups), side_casts=[(w_up, l, None), (w_out, l, None)], name="in_proj_conv")
        bn_gla = 1024
        assert (3 * conv_width) % bn_gla == 0 and (main_cols - 3 * conv_width) % bn_gla == 0
        proj, wq_l, wo_l, *w_in_next = _matmul(
            [hg], [(w_in_l, None, 0)], out_dtype=_BF16, row_ssq=ssq, bn=bn_gla,
            cols=(3 * conv_width // bn_gla, (main_cols - 3 * conv_width) // bn_gla), name="in_proj_gla",
            side_casts=[(wq_x, l, None), (wo_x, l, None)] + ([(w_in_t, l + 1, conv_groups)] if l + 1 < depth else []))
        o_gla = _gla(proj, a_lr, w_gate_all[l], b_gate[l], gla_norm[l], dk=dk, dv=dv)
        h = _matmul([y_conv, o_gla], [(w_out_l, None, 0), (w_out_l, None, conv_width)], out_dtype=_F32, res=h,
                    bk=conv_width, name="out_proj")
        memn = _rmsnorm(mem2, norm_mem[l], _BF16, 256)
        k = _matmul([memn], [(wk_x, l, 0)], out_dtype=_BF16, bn=512, name="xattn_k")
        v = _matmul([memn], [(wv_x, l, 0)], out_dtype=_BF16, bn=512, name="xattn_v")
        h, hg, ssq = _xattn(h, norm_xattn[l], wq_l, wo_l, k, v, norm_mlp[l])
        hid, w_down_l = _matmul([hg], [(w_up_l, None, 0)], out_dtype=_BF16, act="relu2", row_ssq=ssq,
                                side_casts=[(w_down, l, None)], name="mlp_up")
        if l + 1 < depth:
            h, hg, ssq = _matmul([hid], [(w_down_l, None, 0)], out_dtype=_F32, res=h, next_gain=norm_mix[l + 1],
                                 name="mlp_down")
            w_in_l = w_in_next[0]
        else:
            h = _matmul([hid], [(w_down_l, None, 0)], out_dtype=_F32, res=h, name="mlp_down")
    out = _rmsnorm(h, norm_final, _F32, 512)
    return out.reshape(bsz, seq, d)
```

```python
import functools
import math

import jax
import jax.numpy as jnp
import numpy as np
from jax import lax
from jax.experimental import pallas as pl
from jax.experimental.pallas import tpu as pltpu

EPS = 1e-6
GLA_CHUNK = 64
GLA_GROUP = 4
GLA_HEADS = 4
GLA_HEADS_PER_STEP = 4
XATTN_HEADS = 4
GATE_TAU = 16.0
CONV_K = 3
MXU_COLS = 256
CONV_SEG = MXU_COLS
LANES = 128
SUBLANES = 8
MIB = 1024 * 1024
VMEM_CAP = 60 * MIB
VMEM_TEMPS = 12 * MIB
VMEM_DEFAULT = 40 * MIB
LOG2E = math.log2(math.e)

_BF16 = jnp.bfloat16
_F32 = jnp.float32
_NT = (((1,), (1,)), ((), ()))
_TN = (((0,), (0,)), ((), ()))


def _params(sem, block_bytes=None):
    limit = VMEM_DEFAULT if block_bytes is None else min(VMEM_CAP, block_bytes + VMEM_TEMPS)
    return pltpu.CompilerParams(dimension_semantics=sem, vmem_limit_bytes=limit)


def _lane_partial_ssq(h):
    sq = h * h
    acc = sq[:, :LANES]
    for t in range(1, h.shape[1] // LANES):
        acc = acc + sq[:, t * LANES:(t + 1) * LANES]
    return acc


def _row_rsqrt(ssq_ref, inv_d):
    return lax.rsqrt(jnp.sum(ssq_ref[...], axis=-1, keepdims=True) * inv_d + EPS)


def _rmsnorm_body(x_ref, g_ref, o_ref):
    x = x_ref[...]
    y = x * lax.rsqrt(jnp.mean(x * x, axis=-1, keepdims=True) + EPS)
    o_ref[...] = (y * g_ref[...]).astype(o_ref.dtype)


def _rmsnorm(x, gain, out_dtype, bm):
    m, d = x.shape
    bm = min(bm, m)
    return pl.pallas_call(
        _rmsnorm_body,
        out_shape=jax.ShapeDtypeStruct((m, d), out_dtype),
        grid=(m // bm,),
        in_specs=[pl.BlockSpec((bm, d), lambda i: (i, 0)), pl.BlockSpec((1, d), lambda i: (0, 0))],
        out_specs=pl.BlockSpec((bm, d), lambda i: (i, 0)),
        compiler_params=_params(("parallel",)),
        name="rmsnorm",
    )(x, gain.reshape(1, d).astype(_F32))


def _norm_prep_body(x_ref, g_ref, hg_ref, ssq_ref):
    x = x_ref[...]
    hg_ref[...] = (x * g_ref[...]).astype(hg_ref.dtype)
    ssq_ref[...] = _lane_partial_ssq(x)


def _norm_prep(x, gain, bm=512):
    m, d = x.shape
    row_block = lambda width: pl.BlockSpec((bm, width), lambda i: (i, 0))
    return pl.pallas_call(
        _norm_prep_body,
        out_shape=(jax.ShapeDtypeStruct((m, d), _BF16), jax.ShapeDtypeStruct((m, LANES), _F32)),
        grid=(m // bm,),
        in_specs=[row_block(d), pl.BlockSpec((1, d), lambda i: (0, 0))],
        out_specs=(row_block(d), row_block(LANES)),
        compiler_params=_params(("parallel",)),
        name="norm_prep",
    )(x, gain.reshape(1, d).astype(_F32))


def _mm_body(*refs, npairs, grid, act, has_res, inv_d, cast_w, sides, has_conv, emit_norm):
    it = iter(refs)
    xs = [next(it) for _ in range(npairs)]
    ws = [next(it) for _ in range(npairs)]
    res_ref = next(it) if has_res else None
    ssq_in_ref = next(it) if inv_d is not None else None
    wconv_ref, wgate_ref = (next(it), next(it)) if has_conv else (None, None)
    gain_ref = next(it) if emit_norm else None
    side_src = [next(it) for _ in sides]
    o_ref = next(it)
    gate_ref = next(it) if has_conv else None
    hg_ref, ssq_ref = (next(it), next(it)) if emit_norm else (None, None)
    side_dst = [next(it) for _ in sides]
    tail_ref = next(it) if has_conv else None
    nk = grid[2]

    step = (pl.program_id(0) * grid[1] + pl.program_id(1)) * nk + pl.program_id(2)
    for (transposed, nblocks), src, dst in zip(sides, side_src, side_dst):
        def cast(src=src, dst=dst, transposed=transposed):
            val = src[...]
            dst[...] = (val.T if transposed else val).astype(dst.dtype)

        if nblocks == grid[0] * grid[1] * nk:
            cast()
        else:
            pl.when(step < nblocks)(cast)

    bn = ws[0].shape[1]
    chunk = min(MXU_COLS, bn)

    def product(c0):
        cs = slice(c0, c0 + chunk)
        acc = None
        for x_ref, w_ref in zip(xs, ws):
            w = w_ref[:, cs]
            part = jnp.dot(x_ref[...], w.astype(_BF16) if cast_w else w, preferred_element_type=_F32)
            acc = part if acc is None else acc + part
        return acc

    if nk > 1:
        @pl.when(pl.program_id(2) == 0)
        def _():
            o_ref[...] = res_ref[...] if has_res else jnp.zeros_like(o_ref)

        def accumulate():
            for c0 in range(0, bn, chunk):
                o_ref[:, c0:c0 + chunk] += product(c0)

        if not emit_norm:
            accumulate()
            return

        def finish():
            partial = None
            for c0 in range(0, bn, chunk):
                cs = slice(c0, c0 + chunk)
                h = o_ref[:, cs] + product(c0)
                o_ref[:, cs] = h
                hg_ref[:, cs] = (h * gain_ref[:, cs]).astype(hg_ref.dtype)
                part = _lane_partial_ssq(h)
                partial = part if partial is None else partial + part
            first = pl.program_id(1) == 0

            @pl.when(first)
            def _():
                ssq_ref[...] = partial

            @pl.when(jnp.logical_not(first))
            def _():
                ssq_ref[...] += partial

        last = pl.program_id(2) == nk - 1
        pl.when(jnp.logical_not(last))(accumulate)
        pl.when(last)(finish)
        return

    row_scale = _row_rsqrt(ssq_in_ref, inv_d) if inv_d is not None else None

    def finished(c0):
        acc = product(c0)
        if row_scale is not None:
            acc = acc * row_scale
        if act == "relu2":
            r = jnp.maximum(acc, 0.0)
            acc = r * r
        if has_res:
            acc = acc + res_ref[:, c0:c0 + chunk]
        return acc

    if not has_conv:
        for c0 in range(0, bn, chunk):
            o_ref[:, c0:c0 + chunk] = finished(c0).astype(o_ref.dtype)
        return

    j = pl.program_id(1)

    @pl.when(step == 0)
    def _():
        tail_ref[...] = jnp.zeros_like(tail_ref)

    @pl.when(j == 0)
    def _():
        gate_ref[...] = jnp.dot(xs[0][...], wgate_ref[...], preferred_element_type=_F32) * row_scale

    b_gate, c_gate, h_in = (finished(c0) for c0 in range(0, bn, chunk))
    u = c_gate * h_in
    rows = u.shape[0]
    row = lax.broadcasted_iota(jnp.int32, u.shape, 0)
    tail = tail_ref[j]
    prev1 = tail[SUBLANES - 1:SUBLANES, :]
    prev2 = tail[SUBLANES - 2:SUBLANES - 1, :]
    u1 = jnp.where(row == 0, prev1, pltpu.roll(u, 1, axis=0))
    u2 = jnp.where(row == 0, prev2, jnp.where(row == 1, prev1, pltpu.roll(u, 2, axis=0)))
    w = wconv_ref[...]
    y = b_gate * (w[0:1, :] * u2 + w[1:2, :] * u1 + w[2:3, :] * u)
    o_ref[...] = y.astype(o_ref.dtype)
    tail_ref[j] = u[rows - SUBLANES:, :]


def _conv_group_order(blk, nseg):
    return jnp.where(blk < 3 * nseg, 3 * (blk % nseg) + blk // nseg, blk)


def _matmul(xs, ws, *, out_dtype, res=None, act=None, row_ssq=None, next_gain=None, side_casts=(), conv=None,
            cols=None, bm=1024, bn=1024, bk=4096, name="matmul"):
    cast_w = ws[0][0].dtype == _F32
    w_bytes = 4 if cast_w else 2
    npairs = len(xs)
    m, kdim = xs[0].shape
    n = ws[0][0].shape[-1]
    bm, bn, bk = min(bm, m), min(bn, n), min(bk, kdim)
    nk = kdim // bk
    assert m % bm == 0 and kdim % bk == 0
    assert nk == 1 or (act is None and row_ssq is None and out_dtype == _F32)
    assert next_gain is None or nk > 1
    j0, ntiles = cols if cols is not None else (0, n // bn)
    grid = (m // bm, ntiles, nk)
    nsteps = grid[0] * grid[1] * nk
    in_specs = [pl.BlockSpec((bm, bk), lambda i, j, k: (i, k))] * npairs
    for w, layer, row0 in ws:
        assert row0 % bk == 0
        kb = row0 // bk
        if layer is None:
            in_specs.append(pl.BlockSpec((bk, bn), lambda i, j, k, kb=kb: (kb + k, j0 + j)))
        else:
            in_specs.append(pl.BlockSpec((None, bk, bn), lambda i, j, k, kb=kb, layer=layer: (layer, kb + k, j0 + j)))
    args = list(xs) + [w for w, _, _ in ws]
    out_bytes = jnp.dtype(out_dtype).itemsize
    block_bytes = 2 * (npairs * (2 * bm * bk + w_bytes * bk * bn) + bm * bn * out_bytes)
    if res is not None:
        in_specs.append(pl.BlockSpec((bm, bn), lambda i, j, k: (i, j)))
        args.append(res)
        block_bytes += 2 * 4 * bm * bn
    inv_d = None
    if row_ssq is not None:
        in_specs.append(pl.BlockSpec((bm, LANES), lambda i, j, k: (i, 0)))
        args.append(row_ssq)
        inv_d = 1.0 / kdim
        block_bytes += 2 * 4 * bm * LANES
    scratch = []
    if conv is None:
        out_shape = [jax.ShapeDtypeStruct((m, ntiles * bn), out_dtype)]
        out_specs = [pl.BlockSpec((bm, bn), lambda i, j, k: (i, j))]
    else:
        taps, w_gate = conv
        seg = bn // 3
        assert seg == CONV_SEG and nk == 1 and taps.shape[1] == ntiles * seg and row_ssq is not None
        in_specs += [pl.BlockSpec((SUBLANES, seg), lambda i, j, k: (0, j)),
                     pl.BlockSpec(w_gate.shape, lambda i, j, k: (0, 0))]
        args += [taps, w_gate]
        out_shape = [jax.ShapeDtypeStruct((m, ntiles * seg), out_dtype),
                     jax.ShapeDtypeStruct((m, w_gate.shape[1]), _F32)]
        out_specs = [pl.BlockSpec((bm, seg), lambda i, j, k: (i, j)),
                     pl.BlockSpec((bm, w_gate.shape[1]), lambda i, j, k: (i, 0))]
        scratch = [pltpu.VMEM((ntiles, SUBLANES, seg), _F32)]
        block_bytes += 2 * (2 * w_gate.size + 4 * bm * w_gate.shape[1])
    if next_gain is not None:
        assert n == ntiles * bn
        in_specs.append(pl.BlockSpec((1, bn), lambda i, j, k: (0, j)))
        args.append(next_gain.reshape(1, n).astype(_F32))
        out_shape += [jax.ShapeDtypeStruct((m, n), _BF16), jax.ShapeDtypeStruct((m, LANES), _F32)]
        out_specs += [pl.BlockSpec((bm, bn), lambda i, j, k: (i, j)),
                      pl.BlockSpec((bm, LANES), lambda i, j, k: (i, 0))]
        block_bytes += 2 * (2 * bm * bn + 4 * bm * LANES)
    sides = []
    for src, layer, conv_groups in side_casts:
        _, rows, cols = src.shape
        transposed = conv_groups is not None
        if transposed:
            slab = CONV_SEG
            nblocks = pl.cdiv(rows, slab)
            out_shape.append(jax.ShapeDtypeStruct((cols, rows), _BF16))
        else:
            tile = 2 * SUBLANES
            slab = next(c for c in range(tile, rows + 1, tile) if rows % c == 0 and rows // c <= nsteps)
            nblocks = rows // slab
            out_shape.append(jax.ShapeDtypeStruct((rows, cols), _BF16))
        assert nblocks <= nsteps
        blk = lambda i, j, k, last=nblocks - 1: jnp.minimum((i * grid[1] + j) * nk + k, last)
        in_specs.append(pl.BlockSpec((None, slab, cols), lambda i, j, k, layer=layer, blk=blk: (layer, blk(i, j, k), 0)))
        args.append(src)
        if transposed:
            out_specs.append(pl.BlockSpec(
                (cols, slab), lambda i, j, k, blk=blk, g=conv_groups: (0, _conv_group_order(blk(i, j, k), g))))
        else:
            out_specs.append(pl.BlockSpec((slab, cols), lambda i, j, k, blk=blk: (blk(i, j, k), 0)))
        sides.append((transposed, nblocks))
        block_bytes += 2 * (4 + 2) * slab * cols
    body = functools.partial(_mm_body, npairs=npairs, grid=grid, act=act, has_res=res is not None, inv_d=inv_d,
                             cast_w=cast_w, sides=tuple(sides), has_conv=conv is not None,
                             emit_norm=next_gain is not None)
    outs = pl.pallas_call(
        body,
        out_shape=tuple(out_shape),
        grid=grid,
        in_specs=in_specs,
        out_specs=tuple(out_specs),
        scratch_shapes=scratch,
        compiler_params=_params(("arbitrary", "arbitrary", "arbitrary"), block_bytes),
        name=name,
    )(*args)
    return outs[0] if len(outs) == 1 else outs


def _cast_transposed_body(src_ref, dst_ref):
    dst_ref[...] = src_ref[...].T.astype(dst_ref.dtype)


def _cast_transposed(w_t, layer, conv_groups):
    _, n, kdim = w_t.shape
    bn = CONV_SEG
    return pl.pallas_call(
        _cast_transposed_body,
        out_shape=jax.ShapeDtypeStruct((kdim, n), _BF16),
        grid=(pl.cdiv(n, bn),),
        in_specs=[pl.BlockSpec((None, bn, kdim), lambda t: (layer, t, 0))],
        out_specs=pl.BlockSpec((kdim, bn), lambda t: (0, _conv_group_order(t, conv_groups))),
        compiler_params=_params(("parallel",)),
        name="cast_transposed",
    )(w_t)


def _bf16_split_dot(a_exact_bf16, x):
    hi = x.astype(_BF16)
    lo = (x - hi.astype(_F32)).astype(_BF16)
    return (jnp.dot(a_exact_bf16, hi, preferred_element_type=_F32)
            + jnp.dot(a_exact_bf16, lo, preferred_element_type=_F32))


def _gla_constants(L, G):
    nlev = L.bit_length() - 1
    r = np.arange(L * G)
    cum_op = (r[None, :] <= r[:, None]) & (r[None, :] // L == r[:, None] // L)
    i = np.arange(L)
    base = [np.eye(L, dtype=bool)]
    for lev in range(nlev):
        s = L >> lev
        lower = (i % s) >= s // 2
        base.append((i[:, None] // s == i[None, :] // s) & lower[:, None] & ~lower[None, :])
    base = np.stack(base).astype(np.float32)
    masks = np.zeros((nlev + 1, 2, L, 2 * L), np.float32)
    masks[:, 0, :, :L] = base
    masks[:, 1, :, L:] = base
    return jnp.asarray(cum_op, _BF16), jnp.asarray(masks)


def _gla_level_operand(q, k, b, la, s, row8):
    half = s // 2
    n = b.shape[0]
    parts = []
    if half == 1:
        low = (row8 & 1) == 1
        q_dec = q * jnp.exp2(la)
        for r0 in range(0, n, SUBLANES):
            rows = slice(r0, r0 + SUBLANES)
            parts.append(jnp.where(low, q_dec[rows], k[rows]))
    elif half >= SUBLANES:
        for r0 in range(0, n, half):
            rows = slice(r0, r0 + half)
            r_bound = (r0 // s) * s + half - 1
            b_r = b[r_bound:r_bound + 1, :]
            if (r0 // half) % 2:
                parts.append(q[rows] * jnp.exp2(b[rows] - b_r))
            else:
                parts.append(k[rows] * jnp.exp2(b_r - b[rows]))
    else:
        low = (row8 & (s - 1)) >= half
        sign = jnp.where(low, -1.0, 1.0)
        for r0 in range(0, n, SUBLANES):
            rows = slice(r0, r0 + SUBLANES)
            b8 = b[rows]
            b_r = b8[half - 1:half, :]
            for blk in range(1, SUBLANES // s):
                b_r = jnp.where(row8 >= blk * s, b8[blk * s + half - 1:blk * s + half, :], b_r)
            parts.append(jnp.where(low, q[rows], k[rows]) * jnp.exp2((b_r - b8) * sign))
    return jnp.concatenate(parts, axis=0).astype(_BF16)


def _gla_prepare_head(q, k, v, la, cum_op, mask_ref, row8):
    L, G = GLA_CHUNK, GLA_GROUP
    nlev = L.bit_length() - 1
    dk = q.shape[1]
    b = _bf16_split_dot(cum_op, la)
    b_last = [b[(c + 1) * L - 1:(c + 1) * L, :] for c in range(G)]
    q_in = (q * jnp.exp2(b)).astype(_BF16)
    k_dec = jnp.concatenate(
        [k[c * L:(c + 1) * L] * jnp.exp2(b_last[c] - b[c * L:(c + 1) * L]) for c in range(G)], axis=0
    ).astype(_BF16)
    decays = [jnp.exp2(bl) for bl in b_last]

    acc = [None] * G

    def add_level(idx, p):
        for c in range(G):
            t = c // 2
            blk = p[c * L:(c + 1) * L, t * LANES:(t + 1) * LANES] * mask_ref[idx, c % 2]
            acc[c] = blk if acc[c] is None else acc[c] + blk

    add_level(0, lax.dot_general(q.astype(_BF16), k.astype(_BF16), _NT, preferred_element_type=_F32))
    for lev in range(nlev):
        x = _gla_level_operand(q, k, b, la, L >> lev, row8)
        add_level(lev + 1, lax.dot_general(x, x, _NT, preferred_element_type=_F32))

    zeros = jnp.zeros((L, LANES), _F32)
    a_bd = jnp.concatenate(
        [jnp.concatenate([acc[c] if t == c // 2 else zeros for t in range(G // 2)], axis=1) for c in range(G)],
        axis=0).astype(_BF16)
    o_intra = jnp.dot(a_bd, v, preferred_element_type=_F32)
    return o_intra, q_in, k_dec, decays


def _gla_body(q_ref, k_ref, v_ref, g_ref, a_ref, wg_ref, bg_ref, gn_ref, cum_ref, mask_ref, o_ref,
              state_ref, oin_ref, qin_ref, kdec_ref, dec_ref, *, dk, dv):
    L, G = GLA_CHUNK, GLA_GROUP

    @pl.when(pl.program_id(1) == 0)
    def _():
        state_ref[...] = jnp.zeros_like(state_ref)

    z = jnp.dot(a_ref[...].astype(_BF16), wg_ref[...], preferred_element_type=_F32) + bg_ref[...]
    la_all = (jnp.minimum(z, 0.0) - jnp.log(1.0 + jnp.exp(-jnp.abs(z)))) * (LOG2E / GATE_TAU)

    row8 = lax.broadcasted_iota(jnp.int32, (SUBLANES, dk), 0)
    scale = dk ** -0.5
    cum_op = cum_ref[...]
    for hd in range(GLA_HEADS_PER_STEP):
        kc = slice(hd * dk, (hd + 1) * dk)
        vc = slice(hd * dv, (hd + 1) * dv)
        o_intra, q_in, k_dec, decays = _gla_prepare_head(
            q_ref[:, kc].astype(_F32) * scale, k_ref[:, kc].astype(_F32), v_ref[:, vc], la_all[:, kc],
            cum_op, mask_ref, row8)
        oin_ref[:, vc] = o_intra
        qin_ref[:, kc] = q_in
        kdec_ref[:, kc] = k_dec
        for c in range(G):
            dec_ref[c:c + 1, kc] = decays[c]

    for c in range(G):
        rows = slice(c * L, (c + 1) * L)
        for hd in range(GLA_HEADS_PER_STEP):
            kc = slice(hd * dk, (hd + 1) * dk)
            vc = slice(hd * dv, (hd + 1) * dv)
            st = state_ref[hd]
            o = oin_ref[rows, vc] + lax.dot_general(qin_ref[rows, kc], st.astype(_BF16), _NT,
                                                    preferred_element_type=_F32)
            upd = lax.dot_general(v_ref[rows, vc], kdec_ref[rows, kc], _TN, preferred_element_type=_F32)
            state_ref[hd] = st * dec_ref[c:c + 1, kc] + upd
            o = o * lax.rsqrt(jnp.mean(o * o, axis=-1, keepdims=True) + EPS)
            g = g_ref[rows, vc].astype(_F32)
            o_ref[rows, vc] = (o * gn_ref[:, vc] * (g * jax.nn.sigmoid(g))).astype(o_ref.dtype)


def _gla(proj, a_lr, w_gate_pad, b_gate, gla_norm, *, dk, dv):
    s = proj.shape[0]
    hp = GLA_HEADS_PER_STEP
    tb = GLA_CHUNK * GLA_GROUP
    kw, vw = hp * dk, hp * dv
    q_blk = 0
    k_blk = q_blk + GLA_HEADS // hp
    v_blk = (2 * GLA_HEADS * dk) // vw
    g_blk = v_blk + GLA_HEADS // hp
    assert v_blk * vw == 2 * GLA_HEADS * dk and proj.shape[1] == 2 * GLA_HEADS * (dk + dv)
    assert 2 * GLA_CHUNK == LANES and GLA_GROUP % 2 == 0 and s % tb == 0 and GLA_HEADS % hp == 0
    na = a_lr.shape[1]
    cum_op, pair_mask = _gla_constants(GLA_CHUNK, GLA_GROUP)
    body = functools.partial(_gla_body, dk=dk, dv=dv)
    return pl.pallas_call(
        body,
        out_shape=jax.ShapeDtypeStruct((s, GLA_HEADS * dv), _BF16),
        grid=(GLA_HEADS // hp, s // tb),
        in_specs=[pl.BlockSpec((tb, kw), lambda hg, t: (t, q_blk + hg)),
                  pl.BlockSpec((tb, kw), lambda hg, t: (t, k_blk + hg)),
                  pl.BlockSpec((tb, vw), lambda hg, t: (t, v_blk + hg)),
                  pl.BlockSpec((tb, vw), lambda hg, t: (t, g_blk + hg)),
                  pl.BlockSpec((tb, na), lambda hg, t: (t, 0)),
                  pl.BlockSpec((na, kw), lambda hg, t: (0, hg)),
                  pl.BlockSpec((1, kw), lambda hg, t: (0, hg)),
                  pl.BlockSpec((1, vw), lambda hg, t: (0, hg)),
                  pl.BlockSpec(cum_op.shape, lambda hg, t: (0, 0)),
                  pl.BlockSpec(pair_mask.shape, lambda hg, t: (0, 0, 0, 0))],
        out_specs=pl.BlockSpec((tb, vw), lambda hg, t: (t, hg)),
        scratch_shapes=[pltpu.VMEM((hp, dv, dk), _F32),
                        pltpu.VMEM((tb, vw), _F32),
                        pltpu.VMEM((tb, kw), _BF16),
                        pltpu.VMEM((tb, kw), _BF16),
                        pltpu.VMEM((SUBLANES, kw), _F32)],
        compiler_params=_params(("parallel", "arbitrary")),
        name="gla",
    )(proj, proj, proj, proj, a_lr, w_gate_pad, b_gate.reshape(1, -1).astype(_F32),
      gla_norm.reshape(1, -1).astype(_F32), cum_op, pair_mask)


def _xattn_body(res_ref, gin_ref, wq_ref, k_ref, v_ref, wo_ref, gain_ref, o_ref, hgo_ref, ssqo_ref, *, dh):
    scale = dh ** -0.5
    x = res_ref[...]
    r = lax.rsqrt(jnp.mean(x * x, axis=-1, keepdims=True) + EPS)
    q = jnp.dot((x * gin_ref[...]).astype(_BF16), wq_ref[...], preferred_element_type=_F32) * r
    q = q.astype(_BF16)
    heads = []
    for hd in range(XATTN_HEADS):
        sl = slice(hd * dh, (hd + 1) * dh)
        sc = lax.dot_general(q[:, sl], k_ref[:, sl], _NT, preferred_element_type=_F32) * scale
        p = jnp.exp(sc - jnp.max(sc, axis=-1, keepdims=True))
        denom = jnp.sum(p, axis=-1, keepdims=True)
        pv = jnp.dot(p.astype(_BF16), v_ref[:, sl], preferred_element_type=_F32)
        heads.append((pv / denom).astype(_BF16))
    att = jnp.concatenate(heads, axis=1)
    h = jnp.dot(att, wo_ref[...], preferred_element_type=_F32) + x
    o_ref[...] = h
    hgo_ref[...] = (h * gain_ref[...]).astype(hgo_ref.dtype)
    ssqo_ref[...] = _lane_partial_ssq(h)


def _mem_kv_body(mem_ref, g_ref, wk_ref, wv_ref, k_ref, v_ref):
    x = mem_ref[...]
    y = x * lax.rsqrt(jnp.mean(x * x, axis=-1, keepdims=True) + EPS)
    memn = (y * g_ref[...]).astype(_BF16)
    for w_ref, o_ref in ((wk_ref, k_ref), (wv_ref, v_ref)):
        o_ref[...] = jnp.dot(memn, w_ref[...].astype(_BF16), preferred_element_type=_F32).astype(o_ref.dtype)


def _mem_kv(mem, norm_mem, wk, wv, bn=512):
    depth, d, width = wk.shape
    mlen = mem.shape[0]
    w_spec = pl.BlockSpec((None, d, bn), lambda l, j: (l, 0, j))
    o_spec = pl.BlockSpec((None, mlen, bn), lambda l, j: (l, 0, j))
    out = jax.ShapeDtypeStruct((depth, mlen, width), _BF16)
    return pl.pallas_call(
        _mem_kv_body,
        out_shape=(out, out),
        grid=(depth, width // bn),
        in_specs=[pl.BlockSpec((mlen, d), lambda l, j: (0, 0)), pl.BlockSpec((None, 1, d), lambda l, j: (l, 0, 0)),
                  w_spec, w_spec],
        out_specs=(o_spec, o_spec),
        compiler_params=_params(("parallel", "parallel"), 2 * (2 * 4 * d * bn + 4 * mlen * d + 2 * 2 * mlen * bn)),
        name="mem_kv",
    )(mem, norm_mem.reshape(depth, 1, d).astype(_F32), wk, wv)


def _xattn(res, gain, wq, wo, k, v, layer, next_gain, *, bm=256):
    s, d = res.shape
    _, mlen, width = k.shape
    body = functools.partial(_xattn_body, dh=width // XATTN_HEADS)
    row_block = lambda w: pl.BlockSpec((bm, w), lambda i: (i, 0))
    whole = lambda a: pl.BlockSpec((None,) + a.shape[1:], lambda i: (layer, 0, 0))
    w_block = lambda a: pl.BlockSpec(a.shape, lambda i: (0, 0), pipeline_mode=pl.Buffered(1))
    gain_row = pl.BlockSpec((1, d), lambda i: (0, 0))
    block_bytes = (2 * bm * 4 * d + 2 * bm * (4 * d + 2 * d + 4 * LANES)
                   + 2 * 2 * d * width + 2 * 2 * 2 * mlen * width)
    return pl.pallas_call(
        body,
        out_shape=(jax.ShapeDtypeStruct((s, d), _F32), jax.ShapeDtypeStruct((s, d), _BF16),
                   jax.ShapeDtypeStruct((s, LANES), _F32)),
        grid=(s // bm,),
        in_specs=[row_block(d), gain_row, w_block(wq), whole(k), whole(v), w_block(wo), gain_row],
        out_specs=(row_block(d), row_block(d), row_block(LANES)),
        compiler_params=_params(("parallel",), block_bytes),
        name="xattn",
    )(res, gain.reshape(1, d).astype(_F32), wq, k, v, wo, next_gain.reshape(1, d).astype(_F32))


def kernel(x, mem, norm_mix, w_in, w_conv, w_gate_up, b_gate, gla_norm, w_out, norm_xattn, norm_mem,
           wq_x, wk_x, wv_x, wo_x, norm_mlp, w_up, w_down, norm_final):
    bsz, seq, d = x.shape
    depth = w_in.shape[0]
    conv_width = w_conv.shape[2]
    gla_key = w_gate_up.shape[2]
    gla_width = gla_norm.shape[1]
    gate_rank = w_gate_up.shape[1]
    dk = gla_key // GLA_HEADS
    dv = gla_width // GLA_HEADS
    main_cols = 3 * conv_width + 2 * gla_key + 2 * gla_width
    assert bsz == 1 and w_in.shape[2] == main_cols + gate_rank

    w_gate_all = jnp.zeros((depth, LANES, gla_key), _BF16).at[:, :gate_rank].set(w_gate_up.astype(_BF16))
    assert conv_width % CONV_SEG == 0
    conv_groups = conv_width // CONV_SEG
    w_in_t = jnp.swapaxes(w_in, 1, 2)
    w_in_l = _cast_transposed(w_in_t, 0, conv_groups)

    h = x.reshape(seq, d)
    k_all, v_all = _mem_kv(mem.reshape(mem.shape[1], d), norm_mem, wk_x, wv_x)
    hg, ssq = _norm_prep(h, norm_mix[0])
    for l in range(depth):
        w_a = jnp.zeros((d, LANES), _BF16).at[:, :gate_rank].set(w_in_l[:, main_cols:])
        w_taps = jnp.zeros((SUBLANES, conv_width), _F32).at[:CONV_K].set(w_conv[l].astype(_F32))
        y_conv, a_lr, w_up_l, w_out_l = _matmul(
            [hg], [(w_in_l, None, 0)], out_dtype=_BF16, row_ssq=ssq, conv=(w_taps, w_a), bn=3 * CONV_SEG,
            cols=(0, conv_groups), side_casts=[(w_up, l, None), (w_out, l, None)], name="in_proj_conv")
        bn_gla = 1024
        assert (3 * conv_width) % bn_gla == 0 and (main_cols - 3 * conv_width) % bn_gla == 0
        proj, wq_l, wo_l, *w_in_next = _matmul(
            [hg], [(w_in_l, None, 0)], out_dtype=_BF16, row_ssq=ssq, bn=bn_gla,
            cols=(3 * conv_width // bn_gla, (main_cols - 3 * conv_width) // bn_gla), name="in_proj_gla",
            side_casts=[(wq_x, l, None), (wo_x, l, None)] + ([(w_in_t, l + 1, conv_groups)] if l + 1 < depth else []))
        o_gla = _gla(proj, a_lr, w_gate_all[l], b_gate[l], gla_norm[l], dk=dk, dv=dv)
        h = _matmul([y_conv, o_gla], [(w_out_l, None, 0), (w_out_l, None, conv_width)], out_dtype=_F32, res=h,
                    bk=conv_width, name="out_proj")
        h, hg, ssq = _xattn(h, norm_xattn[l], wq_l, wo_l, k_all, v_all, l, norm_mlp[l])
        hid, w_down_l = _matmul([hg], [(w_up_l, None, 0)], out_dtype=_BF16, act="relu2", row_ssq=ssq,
                                side_casts=[(w_down, l, None)], name="mlp_up")
        if l + 1 < depth:
            h, hg, ssq = _matmul([hid], [(w_down_l, None, 0)], out_dtype=_F32, res=h, next_gain=norm_mix[l + 1],
                                 name="mlp_down")
            w_in_l = w_in_next[0]
        else:
            h = _matmul([hid], [(w_down_l, None, 0)], out_dtype=_F32, res=h, name="mlp_down")
    out = _rmsnorm(h, norm_final, _F32, 512)
    return out.reshape(bsz, seq, d)
```
